```python
import jax, jax.numpy as jnp
from jax import lax
import numpy as np

D_MODEL = 2048
BATCH = 2
SEQ = 8192
DEPTH = 4

SSD_HEADS = 32
SSD_HEAD_DIM = 64
SSD_WIDTH = SSD_HEADS * SSD_HEAD_DIM
SSD_GROUPS = 8
SSD_STATE = 128
SSD_CONV = 4
SSD_CHUNK = 128
SSD_XBC = SSD_WIDTH + 2 * SSD_GROUPS * SSD_STATE
SSD_NORM_EPS = 1e-5

S5_GROUP_CH = 16
S5_GROUPS = 64
S5_WIDTH = S5_GROUPS * S5_GROUP_CH
S5_STATE = 64

RWKV_HEADS = 16
RWKV_HEAD_DIM = 64
RWKV_WIDTH = RWKV_HEADS * RWKV_HEAD_DIM
RWKV_LORA = 64
RWKV_LNX_EPS = 64e-5

N_BRANCHES = 3
COL_SIZES = (SSD_WIDTH, SSD_XBC, SSD_HEADS, S5_WIDTH, S5_WIDTH, 4 * RWKV_WIDTH, RWKV_LORA, RWKV_LORA, N_BRANCHES * D_MODEL)
IN_COLS = sum(COL_SIZES)
DEEPNORM_ALPHA = (2 * DEPTH) ** 0.25
DEEPNORM_BETA = (8 * DEPTH) ** -0.25
LN_EPS = 1e-5

kernel_name = 'hybrid_ssd_s5_rwkv7_gated_deepnorm'


def _column_offsets():
    offs, acc = [], 0
    for c in COL_SIZES[:-1]:
        acc += c
        offs.append(acc)
    return offs


def layer_norm(h, g, b):
    hf = h.astype(jnp.float32)
    mu = jnp.mean(hf, -1, keepdims=True)
    var = jnp.mean(jnp.square(hf - mu), -1, keepdims=True)
    return ((hf - mu) * lax.rsqrt(var + LN_EPS) * g + b).astype(h.dtype)


def token_shift(p):
    return jnp.concatenate([jnp.zeros_like(p[:, :1]), p[:, :-1]], axis=1)


def causal_depthwise_conv(u, w, bias):
    k, c = w.shape
    out = lax.conv_general_dilated(u, w.astype(u.dtype)[:, None, :], window_strides=(1,),
                                   padding=[(k - 1, 0)], dimension_numbers=('NWC', 'WIO', 'NWC'),
                                   feature_group_count=c)
    return out + bias


def segsum(a):
    t = a.shape[-1]
    cs = jnp.cumsum(a, -1)
    seg = cs[..., :, None] - cs[..., None, :]
    mask = jnp.tril(jnp.ones((t, t), bool))
    return jnp.where(mask, seg, -jnp.inf)


def ssd_chunked(xh, dt, a, bm, cm):
    b, s, h, p = xh.shape
    g, n = bm.shape[-2:]
    j = h // g
    c, l = s // SSD_CHUNK, SSD_CHUNK
    x = (xh * dt[..., None]).reshape(b, c, l, g, j, p)
    da = (dt * a).reshape(b, c, l, g, j).transpose(0, 3, 4, 1, 2)
    bc = bm.reshape(b, c, l, g, n)
    cc = cm.reshape(b, c, l, g, n)
    cs = jnp.cumsum(da, -1)
    lmat = jnp.exp(segsum(da))
    cb = jnp.einsum('bclgn,bcsgn->bcgls', cc, bc)
    y_diag = jnp.einsum('bcgls,bgjcls,bcsgjp->bclgjp', cb, lmat, x)
    decay_states = jnp.exp(cs[..., -1:] - cs)
    states = jnp.einsum('bclgn,bgjcl,bclgjp->bcgjpn', bc, decay_states, x)
    chunk_tot = jnp.pad(cs[..., -1], ((0, 0), (0, 0), (0, 0), (1, 0)))
    decay_chunk = jnp.exp(segsum(chunk_tot))
    states = jnp.concatenate([jnp.zeros_like(states[:, :1]), states], axis=1)
    states = jnp.einsum('bgjzc,bcgjpn->bzgjpn', decay_chunk, states)[:, :-1]
    y_off = jnp.einsum('bclgn,bcgjpn,bgjcl->bclgjp', cc, states, jnp.exp(cs))
    return (y_diag + y_off).reshape(b, s, h, p)


def mamba2_branch(z, xbc, dt_raw, conv_w, conv_b, dt_bias, a_log, d_skip, norm_w):
    b, s, _ = z.shape
    xbc = jax.nn.silu(causal_depthwise_conv(xbc, conv_w, conv_b))
    xs, bm, cm = jnp.split(xbc, [SSD_WIDTH, SSD_WIDTH + SSD_GROUPS * SSD_STATE], axis=-1)
    xh = xs.reshape(b, s, SSD_HEADS, SSD_HEAD_DIM)
    bm = bm.reshape(b, s, SSD_GROUPS, SSD_STATE)
    cm = cm.reshape(b, s, SSD_GROUPS, SSD_STATE)
    dt = jax.nn.softplus(dt_raw + dt_bias)
    a = -jnp.exp(a_log.astype(jnp.float32))
    y = ssd_chunked(xh, dt, a, bm, cm) + d_skip[:, None] * xh
    y = y.reshape(b, s, SSD_WIDTH) * jax.nn.silu(z)
    yg = y.reshape(b, s, SSD_GROUPS, SSD_WIDTH // SSD_GROUPS)
    yg = yg * lax.rsqrt(jnp.mean(jnp.square(yg), -1, keepdims=True) + SSD_NORM_EPS)
    return yg.reshape(b, s, SSD_WIDTH) * norm_w


def s5_branch(u, zb, lam_re, lam_im, log_dt, b_re, b_im, c_re, c_im, d_s5, w_glu, b_glu):
    b, s, _ = u.shape
    ug = u.reshape(b, s, S5_GROUPS, S5_GROUP_CH)
    dt = jnp.exp(log_dt)[:, None]
    mag = jnp.exp(lam_re * dt)
    ang = lam_im * dt
    lb_re, lb_im = mag * jnp.cos(ang), mag * jnp.sin(ang)
    den = jnp.square(lam_re) + jnp.square(lam_im)
    nr, ni = lb_re - 1.0, lb_im
    q_re = (nr * lam_re + ni * lam_im) / den
    q_im = (ni * lam_re - nr * lam_im) / den
    bb_re = q_re[..., None] * b_re - q_im[..., None] * b_im
    bb_im = q_re[..., None] * b_im + q_im[..., None] * b_re
    bu_re = jnp.einsum('bsgh,gph->bsgp', ug, bb_re)
    bu_im = jnp.einsum('bsgh,gph->bsgp', ug, bb_im)
    a_re = jnp.broadcast_to(lb_re, bu_re.shape)
    a_im = jnp.broadcast_to(lb_im, bu_im.shape)

    def combine(e1, e2):
        a1r, a1i, b1r, b1i = e1
        a2r, a2i, b2r, b2i = e2
        return (a2r * a1r - a2i * a1i, a2r * a1i + a2i * a1r,
                a2r * b1r - a2i * b1i + b2r, a2r * b1i + a2i * b1r + b2i)

    _, _, st_re, st_im = lax.associative_scan(combine, (a_re, a_im, bu_re, bu_im), axis=1)
    y = jnp.einsum('bsgp,ghp->bsgh', st_re, c_re) - jnp.einsum('bsgp,ghp->bsgh', st_im, c_im)
    y = y.reshape(b, s, S5_WIDTH) + d_s5 * u
    y = jax.nn.gelu(y)
    y = y * jax.nn.sigmoid(y @ w_glu + b_glu)
    return y * jax.nn.silu(zb)


def rwkv7_recurrence(r, w, k, v, a, bvec):
    b, s, h, n = r.shape

    def step(state, inp):
        r_t, w_t, k_t, v_t, a_t, b_t = inp
        sa = jnp.einsum('bhvk,bhk->bhv', state, a_t)
        state = (state * w_t[:, :, None, :] + sa[..., None] * b_t[:, :, None, :]
                 + v_t[..., None] * k_t[:, :, None, :])
        return state, jnp.einsum('bhvk,bhk->bhv', state, r_t)

    xs = tuple(jnp.moveaxis(t, 1, 0) for t in (r, w, k, v, a, bvec))
    s0 = jnp.zeros((b, h, n, n), jnp.float32)
    _, ys = lax.scan(step, s0, xs)
    return jnp.moveaxis(ys, 0, 1)


def rwkv7_branch(p_rkvz, p_w, p_a, mu, mu_lora, w0, w2, a0, a2, k_k, k_a, r_k, lnx_w, lnx_b):
    b, s, _ = p_rkvz.shape
    heads = lambda t: t.reshape(b, s, RWKV_HEADS, RWKV_HEAD_DIM)
    p4 = p_rkvz.reshape(b, s, 4, RWKV_WIDTH)
    p4 = p4 + (token_shift(p4) - p4) * mu
    r, k, v, zc = p4[:, :, 0], p4[:, :, 1], p4[:, :, 2], p4[:, :, 3]
    xw = p_w + (token_shift(p_w) - p_w) * mu_lora[0]
    xa = p_a + (token_shift(p_a) - p_a) * mu_lora[1]
    w_log = -jax.nn.softplus(-(w0 + jnp.tanh(xw) @ w2)) - 0.5
    decay = jnp.exp(-jnp.exp(w_log))
    a = jax.nn.sigmoid(a0 + xa @ a2)
    kk = heads(k * k_k)
    kk = kk * lax.rsqrt(jnp.maximum(jnp.sum(jnp.square(kk), -1, keepdims=True), 1e-24))
    k = k * (1.0 + (a - 1.0) * k_a)
    rh, kh, vh = heads(r), heads(k), heads(v)
    y = rwkv7_recurrence(rh, heads(decay), kh, vh, -kk, kk * heads(a))
    mean = jnp.mean(y, -1, keepdims=True)
    var = jnp.mean(jnp.square(y - mean), -1, keepdims=True)
    y = ((y - mean) * lax.rsqrt(var + RWKV_LNX_EPS)).reshape(b, s, RWKV_WIDTH) * lnx_w + lnx_b
    y = y + (jnp.sum(rh * kh * r_k, -1, keepdims=True) * vh).reshape(b, s, RWKV_WIDTH)
    return y * jax.nn.silu(zc)


def setup_inputs(seed: int = 0) -> dict:
    key = jax.random.key(seed)
    ks = iter(jax.random.split(key, 48))
    nrm = lambda shape, sc: sc * jax.random.normal(next(ks), shape, jnp.float32)
    uni = lambda shape, lo, hi: jax.random.uniform(next(ks), shape, jnp.float32, lo, hi)
    L = DEPTH
    x = nrm((BATCH, SEQ, D_MODEL), 1.0)
    w_in = nrm((L, D_MODEL, IN_COLS), D_MODEL ** -0.5)
    ssd_conv_w = nrm((L, SSD_CONV, SSD_XBC), SSD_CONV ** -0.5)
    ssd_conv_b = nrm((L, SSD_XBC), 0.02)
    dt0 = jnp.exp(uni((L, SSD_HEADS), float(np.log(1e-3)), float(np.log(1e-1))))
    ssd_dt_bias = dt0 + jnp.log(-jnp.expm1(-dt0))
    ssd_a_log = jnp.log(uni((L, SSD_HEADS), 1.0, 16.0))
    ssd_d = 1.0 + nrm((L, SSD_HEADS), 0.1)
    ssd_norm_w = 1.0 + nrm((L, SSD_WIDTH), 0.05)
    n_idx = jnp.arange(S5_STATE, dtype=jnp.float32)
    s5_lambda_re = -0.5 + nrm((L, S5_GROUPS, S5_STATE), 0.01)
    s5_lambda_im = jnp.pi * n_idx + nrm((L, S5_GROUPS, S5_STATE), 0.01)
    s5_log_dt = uni((L, S5_GROUPS), float(np.log(1e-3)), float(np.log(1e-1)))
    s5_b_re = nrm((L, S5_GROUPS, S5_STATE, S5_GROUP_CH), (2 * S5_GROUP_CH) ** -0.5)
    s5_b_im = nrm((L, S5_GROUPS, S5_STATE, S5_GROUP_CH), (2 * S5_GROUP_CH) ** -0.5)
    s5_c_re = nrm((L, S5_GROUPS, S5_GROUP_CH, S5_STATE), (2 * S5_STATE) ** -0.5)
    s5_c_im = nrm((L, S5_GROUPS, S5_GROUP_CH, S5_STATE), (2 * S5_STATE) ** -0.5)
    s5_d = nrm((L, S5_WIDTH), 1.0)
    s5_w_glu = nrm((L, S5_WIDTH, S5_WIDTH), S5_WIDTH ** -0.5)
    s5_b_glu = nrm((L, S5_WIDTH), 0.02)
    rwkv_mu = uni((L, 4, RWKV_WIDTH), 0.0, 1.0)
    rwkv_mu_lora = uni((L, 2, RWKV_LORA), 0.0, 1.0)
    ratio = jnp.arange(L, dtype=jnp.float32) / max(L - 1, 1)
    ch = jnp.arange(RWKV_WIDTH, dtype=jnp.float32) / (RWKV_WIDTH - 1)
    rwkv_w0 = -6.5 + 5.0 * ch[None, :] ** (0.85 + jnp.sqrt(ratio)[:, None]) + nrm((L, RWKV_WIDTH), 0.1)
    rwkv_w2 = nrm((L, RWKV_LORA, RWKV_WIDTH), 0.1 * RWKV_LORA ** -0.5)
    rwkv_a0 = nrm((L, RWKV_WIDTH), 0.1)
    rwkv_a2 = nrm((L, RWKV_LORA, RWKV_WIDTH), 0.1 * RWKV_LORA ** -0.5)
    rwkv_k_k = 0.85 + nrm((L, RWKV_WIDTH), 0.05)
    rwkv_k_a = 1.0 + nrm((L, RWKV_WIDTH), 0.05)
    rwkv_r_k = nrm((L, RWKV_HEADS, RWKV_HEAD_DIM), 0.1)
    rwkv_lnx_w = 1.0 + nrm((L, RWKV_WIDTH), 0.05)
    rwkv_lnx_b = nrm((L, RWKV_WIDTH), 0.02)
    gate_b = nrm((L, N_BRANCHES, D_MODEL), 0.02)
    w_branch_a = nrm((L, SSD_WIDTH, D_MODEL), DEEPNORM_BETA * SSD_WIDTH ** -0.5)
    w_branch_b = nrm((L, S5_WIDTH, D_MODEL), DEEPNORM_BETA * S5_WIDTH ** -0.5)
    w_branch_c = nrm((L, RWKV_WIDTH, D_MODEL), DEEPNORM_BETA * RWKV_WIDTH ** -0.5)
    w_out = nrm((L, D_MODEL, D_MODEL), DEEPNORM_BETA * D_MODEL ** -0.5)
    ln_g = 1.0 + nrm((L, D_MODEL), 0.05)
    ln_b = nrm((L, D_MODEL), 0.02)
    return {'x': x, 'w_in': w_in, 'ssd_conv_w': ssd_conv_w, 'ssd_conv_b': ssd_conv_b,
            'ssd_dt_bias': ssd_dt_bias, 'ssd_a_log': ssd_a_log, 'ssd_d': ssd_d, 'ssd_norm_w': ssd_norm_w,
            's5_lambda_re': s5_lambda_re, 's5_lambda_im': s5_lambda_im, 's5_log_dt': s5_log_dt,
            's5_b_re': s5_b_re, 's5_b_im': s5_b_im, 's5_c_re': s5_c_re, 's5_c_im': s5_c_im,
            's5_d': s5_d, 's5_w_glu': s5_w_glu, 's5_b_glu': s5_b_glu,
            'rwkv_mu': rwkv_mu, 'rwkv_mu_lora': rwkv_mu_lora, 'rwkv_w0': rwkv_w0, 'rwkv_w2': rwkv_w2,
            'rwkv_a0': rwkv_a0, 'rwkv_a2': rwkv_a2, 'rwkv_k_k': rwkv_k_k, 'rwkv_k_a': rwkv_k_a,
            'rwkv_r_k': rwkv_r_k, 'rwkv_lnx_w': rwkv_lnx_w, 'rwkv_lnx_b': rwkv_lnx_b,
            'gate_b': gate_b, 'w_branch_a': w_branch_a, 'w_branch_b': w_branch_b, 'w_branch_c': w_branch_c,
            'w_out': w_out, 'ln_g': ln_g, 'ln_b': ln_b}


def reference(x, w_in, ssd_conv_w, ssd_conv_b, ssd_dt_bias, ssd_a_log, ssd_d, ssd_norm_w,
              s5_lambda_re, s5_lambda_im, s5_log_dt, s5_b_re, s5_b_im, s5_c_re, s5_c_im,
              s5_d, s5_w_glu, s5_b_glu,
              rwkv_mu, rwkv_mu_lora, rwkv_w0, rwkv_w2, rwkv_a0, rwkv_a2, rwkv_k_k, rwkv_k_a,
              rwkv_r_k, rwkv_lnx_w, rwkv_lnx_b,
              gate_b, w_branch_a, w_branch_b, w_branch_c, w_out, ln_g, ln_b):
    b, s, d = x.shape
    offs = _column_offsets()
    for l in range(DEPTH):
        proj = jnp.einsum('bsd,dc->bsc', x, w_in[l]).astype(jnp.float32)
        z_a, xbc, dt_raw, u_b, z_b, p_rkvz, p_w, p_a, p_gate = jnp.split(proj, offs, axis=-1)
        y_a = mamba2_branch(z_a, xbc, dt_raw, ssd_conv_w[l], ssd_conv_b[l], ssd_dt_bias[l],
                            ssd_a_log[l], ssd_d[l], ssd_norm_w[l])
        y_b = s5_branch(u_b, z_b, s5_lambda_re[l], s5_lambda_im[l], s5_log_dt[l], s5_b_re[l],
                        s5_b_im[l], s5_c_re[l], s5_c_im[l], s5_d[l], s5_w_glu[l], s5_b_glu[l])
        y_c = rwkv7_branch(p_rkvz, p_w, p_a, rwkv_mu[l], rwkv_mu_lora[l], rwkv_w0[l], rwkv_w2[l],
                           rwkv_a0[l], rwkv_a2[l], rwkv_k_k[l], rwkv_k_a[l], rwkv_r_k[l],
                           rwkv_lnx_w[l], rwkv_lnx_b[l])
        gates = jax.nn.sigmoid(p_gate.reshape(b, s, N_BRANCHES, d) + gate_b[l])
        merged = (gates[:, :, 0] * (y_a @ w_branch_a[l])
                  + gates[:, :, 1] * (y_b @ w_branch_b[l])
                  + gates[:, :, 2] * (y_c @ w_branch_c[l]))
        out = (merged @ w_out[l]).astype(x.dtype)
        x = layer_norm(DEEPNORM_ALPHA * x + out, ln_g[l], ln_b[l])
    return x
```

```python
import functools

import jax
import jax.numpy as jnp
import numpy as np
from jax import lax
from jax.experimental import pallas as pl
from jax.experimental.pallas import tpu as pltpu

F32 = jnp.float32
BF16 = jnp.bfloat16
HIGHEST = lax.Precision.HIGHEST

LANES = 128
SUBLANES = 8
VMEM_LIMIT = 56 * 1024 * 1024

D_MODEL = 2048
DEPTH = 4
SSD_HEADS = 32
SSD_HEAD_DIM = 64
SSD_WIDTH = SSD_HEADS * SSD_HEAD_DIM
SSD_GROUPS = 8
SSD_STATE = 128
SSD_CONV = 4
SSD_CHUNK = 128
SSD_NORM_EPS = 1e-5
SSD_GROUP_W = SSD_WIDTH // SSD_GROUPS
S5_GROUP_CH = 16
S5_GROUPS = 64
S5_WIDTH = S5_GROUPS * S5_GROUP_CH
S5_STATE = 64
S5_OCTETS = S5_WIDTH // LANES
S5_OCT_STATE = (LANES // S5_GROUP_CH) * S5_STATE
RWKV_HEADS = 16
RWKV_HEAD_DIM = 64
RWKV_WIDTH = RWKV_HEADS * RWKV_HEAD_DIM
RWKV_LORA = 64
RWKV_LNX_EPS = 64e-5
RWKV_CHUNK = 64
N_BRANCHES = 3
DEEPNORM_ALPHA = (2 * DEPTH) ** 0.25
LN_EPS = 1e-5

OFF_A = 0
LEN_A = SSD_WIDTH + (SSD_WIDTH + 2 * SSD_GROUPS * SSD_STATE) + SSD_HEADS
OFF_B = OFF_A + LEN_A
LEN_B = 2 * S5_WIDTH
OFF_C = OFF_B + LEN_B
LEN_C = 4 * RWKV_WIDTH + 2 * RWKV_LORA
OFF_G = OFF_C + LEN_C
LEN_G = N_BRANCHES * D_MODEL
PAD_A = (-LEN_A) % LANES


def _silu(x):
    return x / (1.0 + jnp.exp(-x))


def _sigmoid(x):
    return 1.0 / (1.0 + jnp.exp(-x))


def _softplus(x):
    return jnp.maximum(x, 0.0) + jnp.log(1.0 + jnp.exp(-jnp.abs(x)))


def _dot(a, b):
    return jnp.dot(a, b, preferred_element_type=F32)


def _dot_nt(a, b):
    return lax.dot_general(a, b, (((1,), (1,)), ((), ())), preferred_element_type=F32)


def _dot_hilo(a, b_bf):
    hi = a.astype(BF16)
    lo = (a - hi.astype(F32)).astype(BF16)
    return _dot(hi, b_bf) + _dot(lo, b_bf)


def _shift_rows(x, carry8, d, row8):
    sh = pltpu.roll(x, d, 0)
    head = jnp.where(row8 < d, pltpu.roll(carry8, d, 0), sh[0:SUBLANES])
    return jnp.concatenate([head, sh[SUBLANES:]], axis=0)


def _mm_kernel(x_ref, w_ref, o_ref):
    o_ref[...] = _dot(x_ref[...], w_ref[...]).astype(o_ref.dtype)


def _project(x_bf, w_bf, tm, tn):
    nt, d = x_bf.shape
    n = w_bf.shape[1]
    return pl.pallas_call(
        _mm_kernel,
        grid=(nt // tm, n // tn),
        in_specs=[pl.BlockSpec((tm, d), lambda i, j: (i, 0)),
                  pl.BlockSpec((d, tn), lambda i, j: (0, j))],
        out_specs=pl.BlockSpec((tm, tn), lambda i, j: (i, j)),
        out_shape=jax.ShapeDtypeStruct((nt, n), BF16),
        compiler_params=pltpu.CompilerParams(
            dimension_semantics=("parallel", "arbitrary"), vmem_limit_bytes=VMEM_LIMIT),
        name="in_proj",
    )(x_bf, w_bf)


def _ssd_kernel(z_ref, xs_ref, bc_ref, dt_ref, cwx_ref, cbx_ref, cwbc_ref, cbbc_ref,
                dtb_ref, a_ref, dsk_ref, nw_ref, exp_ref, tri_ref,
                y_ref, state_ref, cx_ref, cbc_ref, *, tb):
    @pl.when(pl.program_id(1) == 0)
    def _():
        state_ref[...] = jnp.zeros_like(state_ref)
        cx_ref[...] = jnp.zeros_like(cx_ref)
        cbc_ref[...] = jnp.zeros_like(cbc_ref)

    row8 = lax.broadcasted_iota(jnp.int32, (SUBLANES, 1), 0)

    def conv(u_ref, carry_ref, w_ref, b_ref):
        u = u_ref[...].astype(F32)
        c8 = carry_ref[...]
        acc = u * w_ref[SSD_CONV - 1:SSD_CONV, :] + b_ref[...]
        for d in range(1, SSD_CONV):
            acc = acc + _shift_rows(u, c8, d, row8) * w_ref[SSD_CONV - 1 - d:SSD_CONV - d, :]
        carry_ref[...] = u[tb - SUBLANES:tb]
        return _silu(acc)

    xs = conv(xs_ref, cx_ref, cwx_ref, cbx_ref)
    bc = conv(bc_ref, cbc_ref, cwbc_ref, cbbc_ref)

    L = SSD_CHUNK
    ti = lax.broadcasted_iota(jnp.int32, (L, L), 0)
    si = lax.broadcasted_iota(jnp.int32, (L, L), 1)
    causal = ti >= si
    lane = lax.broadcasted_iota(jnp.int32, (1, LANES), 1)
    lo_half = lane < SSD_HEAD_DIM
    tri = tri_ref[...]
    expand = exp_ref[...]
    gsn = SSD_GROUPS * SSD_STATE

    for ci in range(tb // L):
        r0 = ci * L
        dt = _softplus(dt_ref[r0:r0 + L, :].astype(F32) + dtb_ref[...])
        da = dt * a_ref[...]
        cs = jnp.dot(tri, da, precision=HIGHEST, preferred_element_type=F32)
        cs_t = cs.T
        dt_t = dt.T
        e_exp = _dot_hilo(jnp.exp(cs), expand)
        for g in range(SSD_GROUPS):
            c0 = g * SSD_GROUP_W
            x_g = xs[r0:r0 + L, c0:c0 + SSD_GROUP_W]
            b_g = bc[r0:r0 + L, g * SSD_STATE:(g + 1) * SSD_STATE]
            c_g = bc[r0:r0 + L, gsn + g * SSD_STATE:gsn + (g + 1) * SSD_STATE]
            c_bf = c_g.astype(BF16)
            cb = _dot_nt(c_bf, b_g.astype(BF16))
            b_t = b_g.T
            st_old = state_ref[:, c0:c0 + SSD_GROUP_W]
            y_off = _dot(c_bf, st_old.astype(BF16))
            y_parts, new_parts = [], []
            for pr in range(2):
                xp = x_g[:, pr * LANES:(pr + 1) * LANES]
                x_half = (jnp.where(lo_half, xp, 0.0).astype(BF16), jnp.where(lo_half, 0.0, xp).astype(BF16))
                y_p = None
                n_p = None
                for k in range(2):
                    h = g * 4 + pr * 2 + k
                    col = cs[:, h:h + 1]
                    row = cs_t[h:h + 1, :]
                    dtrow = dt_t[h:h + 1, :]
                    lmat = jnp.exp(jnp.where(causal, col - row, -1e30))
                    m_h = (cb * lmat * dtrow).astype(BF16)
                    y_k = _dot(m_h, x_half[k])
                    wrow = jnp.exp(cs_t[h:h + 1, L - 1:L] - row) * dtrow
                    n_k = _dot((b_t * wrow).astype(BF16), x_half[k])
                    y_p = y_k if y_p is None else y_p + y_k
                    n_p = n_k if n_p is None else n_p + n_k
                y_parts.append(y_p)
                new_parts.append(n_p)
            y_diag = jnp.concatenate(y_parts, axis=1)
            new_g = jnp.concatenate(new_parts, axis=1)
            e_g = e_exp[:, c0:c0 + SSD_GROUP_W]
            state_ref[:, c0:c0 + SSD_GROUP_W] = st_old * e_g[L - 1:L, :] + new_g
            y = y_diag + y_off * e_g + dsk_ref[:, c0:c0 + SSD_GROUP_W] * x_g
            y = y * _silu(z_ref[r0:r0 + L, c0:c0 + SSD_GROUP_W].astype(F32))
            ms = jnp.mean(y * y, axis=-1, keepdims=True)
            y = y * lax.rsqrt(ms + SSD_NORM_EPS) * nw_ref[:, c0:c0 + SSD_GROUP_W]
            y_ref[r0:r0 + L, c0:c0 + SSD_GROUP_W] = y.astype(y_ref.dtype)


def _ssd_branch(proj_a, batch, seq, conv_w, conv_b, dt_bias, a_log, d_skip, norm_w, tb):
    nt = batch * seq
    nblk = seq // tb
    w2 = SSD_WIDTH // D_MODEL
    assert w2 == 1
    dt_blk = (2 * SSD_WIDTH + 2 * SSD_GROUPS * SSD_STATE) // LANES
    pad_h = LANES - SSD_HEADS
    dtb = jnp.pad(dt_bias, (0, pad_h)).reshape(1, LANES)
    a_neg = jnp.pad(-jnp.exp(a_log.astype(F32)), (0, pad_h)).reshape(1, LANES)
    dsk = jnp.repeat(d_skip, SSD_HEAD_DIM).reshape(1, SSD_WIDTH)
    expand = (jnp.arange(LANES)[:, None] == (jnp.arange(SSD_WIDTH)[None, :] // SSD_HEAD_DIM)).astype(BF16)
    tri = jnp.tril(jnp.ones((SSD_CHUNK, SSD_CHUNK), F32))
    row = lambda b, i: (b * nblk + i, 0)
    full = lambda b, i: (0, 0)
    return pl.pallas_call(
        functools.partial(_ssd_kernel, tb=tb),
        grid=(batch, nblk),
        in_specs=[
            pl.BlockSpec((tb, SSD_WIDTH), lambda b, i: (b * nblk + i, 0)),
            pl.BlockSpec((tb, SSD_WIDTH), lambda b, i: (b * nblk + i, 1)),
            pl.BlockSpec((tb, SSD_WIDTH), lambda b, i: (b * nblk + i, 2)),
            pl.BlockSpec((tb, LANES), lambda b, i: (b * nblk + i, dt_blk)),
            pl.BlockSpec((SSD_CONV, SSD_WIDTH), lambda b, i: (0, 0)),
            pl.BlockSpec((1, SSD_WIDTH), lambda b, i: (0, 0)),
            pl.BlockSpec((SSD_CONV, SSD_WIDTH), lambda b, i: (0, 1)),
            pl.BlockSpec((1, SSD_WIDTH), lambda b, i: (0, 1)),
            pl.BlockSpec((1, LANES), full), pl.BlockSpec((1, LANES), full),
            pl.BlockSpec((1, SSD_WIDTH), full), pl.BlockSpec((1, SSD_WIDTH), full),
            pl.BlockSpec((LANES, SSD_WIDTH), full), pl.BlockSpec((SSD_CHUNK, SSD_CHUNK), full),
        ],
        out_specs=pl.BlockSpec((tb, SSD_WIDTH), row),
        out_shape=jax.ShapeDtypeStruct((nt, SSD_WIDTH), BF16),
        scratch_shapes=[pltpu.VMEM((SSD_STATE, SSD_WIDTH), F32),
                        pltpu.VMEM((SUBLANES, SSD_WIDTH), F32),
                        pltpu.VMEM((SUBLANES, SSD_WIDTH), F32)],
        compiler_params=pltpu.CompilerParams(
            dimension_semantics=("parallel", "arbitrary"), vmem_limit_bytes=VMEM_LIMIT),
        name="ssd_branch",
    )(proj_a, proj_a, proj_a, proj_a, conv_w, conv_b.reshape(1, -1), conv_w, conv_b.reshape(1, -1),
      dtb, a_neg, dsk, norm_w.reshape(1, -1), expand, tri)


S5_QUADS = 4
S5_BLOCKS = 2 * S5_GROUPS * S5_STATE // LANES
S5_QBLK = S5_BLOCKS // S5_QUADS


def _s5_kernel(u_ref, z_ref, bb_ref, co_ref, lr_ref, li_ref, d_ref, wglu_ref, bglu_ref,
               y_ref, scr_ref, st_ref, *, tb, tpad):
    @pl.when(pl.program_id(1) == 0)
    def _():
        st_ref[...] = jnp.zeros_like(st_ref)

    half_blk = S5_OCT_STATE // LANES
    u_bf = u_ref[...]
    for o in range(S5_OCTETS):
        bu = _dot(u_bf[:, o * LANES:(o + 1) * LANES], bb_ref[o])
        q, half = divmod(o, 2)
        for j in range(half_blk):
            jr = S5_QBLK * q + half_blk * half + j
            ji = jr + SUBLANES
            scr_ref[jr * tpad:jr * tpad + tb, :] = bu[:, j * LANES:(j + 1) * LANES]
            scr_ref[ji * tpad:ji * tpad + tb, :] = bu[:, S5_OCT_STATE + j * LANES:S5_OCT_STATE + (j + 1) * LANES]

    lr = [lr_ref[q] for q in range(S5_QUADS)]
    li = [li_ref[q] for q in range(S5_QUADS)]

    def step(t, carry):
        new = []
        for q in range(S5_QUADS):
            sr, si = carry[2 * q], carry[2 * q + 1]
            re_rows = pl.ds(S5_QBLK * q * tpad + t, SUBLANES, stride=tpad)
            im_rows = pl.ds((S5_QBLK * q + SUBLANES) * tpad + t, SUBLANES, stride=tpad)
            nr = lr[q] * sr - li[q] * si + scr_ref[re_rows, :]
            ni = lr[q] * si + li[q] * sr + scr_ref[im_rows, :]
            scr_ref[re_rows, :] = nr
            scr_ref[im_rows, :] = ni
            new += [nr, ni]
        return tuple(new)

    carry = lax.fori_loop(0, tb, step, tuple(st_ref[i] for i in range(2 * S5_QUADS)), unroll=8)
    for i in range(2 * S5_QUADS):
        st_ref[i] = carry[i]

    ys = []
    for o in range(S5_OCTETS):
        q, half = divmod(o, 2)
        blks = []
        for part in range(2):
            for j in range(half_blk):
                jb = S5_QBLK * q + SUBLANES * part + half_blk * half + j
                blks.append(scr_ref[jb * tpad:jb * tpad + tb, :].astype(BF16))
        ys.append(_dot(jnp.concatenate(blks, axis=1), co_ref[o]))
    y = jnp.concatenate(ys, axis=1) + d_ref[...] * u_bf.astype(F32)
    y = 0.5 * y * (1.0 + jnp.tanh(0.7978845608028654 * (y + 0.044715 * (y * y * y))))
    gl = _dot(y.astype(BF16), wglu_ref[...]) + bglu_ref[...]
    y = y * _sigmoid(gl)
    y_ref[...] = (y * _silu(z_ref[...].astype(F32))).astype(y_ref.dtype)


def _s5_branch(proj_b, batch, seq, lam_re, lam_im, log_dt, b_re, b_im, c_re, c_im, d_s5, w_glu, b_glu, tb):
    nt = batch * seq
    nblk = seq // tb
    tpad = tb + SUBLANES
    dt = jnp.exp(log_dt)[:, None]
    mag = jnp.exp(lam_re * dt)
    ang = lam_im * dt
    lb_re, lb_im = mag * jnp.cos(ang), mag * jnp.sin(ang)
    den = jnp.square(lam_re) + jnp.square(lam_im)
    nr, ni = lb_re - 1.0, lb_im
    q_re = (nr * lam_re + ni * lam_im) / den
    q_im = (ni * lam_re - nr * lam_im) / den
    bb_re = q_re[..., None] * b_re - q_im[..., None] * b_im
    bb_im = q_re[..., None] * b_im + q_im[..., None] * b_re
    g8 = LANES // S5_GROUP_CH
    eye = jnp.eye(g8, dtype=F32)

    def in_mat(bb):
        m = jnp.einsum('ogph,gk->oghkp', bb.reshape(S5_OCTETS, g8, S5_STATE, S5_GROUP_CH), eye)
        return m.reshape(S5_OCTETS, LANES, S5_OCT_STATE)

    def out_mat(c):
        m = jnp.einsum('oghp,gk->okpgh', c.reshape(S5_OCTETS, g8, S5_GROUP_CH, S5_STATE), eye)
        return m.reshape(S5_OCTETS, S5_OCT_STATE, LANES)

    bb = jnp.concatenate([in_mat(bb_re), in_mat(bb_im)], axis=2).astype(BF16)
    co = jnp.concatenate([out_mat(c_re), -out_mat(c_im)], axis=1).astype(BF16)
    lr = lb_re.reshape(S5_QUADS, SUBLANES, LANES)
    li = lb_im.reshape(S5_QUADS, SUBLANES, LANES)
    full2 = lambda b, i: (0, 0)
    full3 = lambda b, i: (0, 0, 0)
    return pl.pallas_call(
        functools.partial(_s5_kernel, tb=tb, tpad=tpad),
        grid=(batch, nblk),
        in_specs=[
            pl.BlockSpec((tb, S5_WIDTH), lambda b, i: (b * nblk + i, 0)),
            pl.BlockSpec((tb, S5_WIDTH), lambda b, i: (b * nblk + i, 1)),
            pl.BlockSpec((S5_OCTETS, LANES, 2 * S5_OCT_STATE), full3),
            pl.BlockSpec((S5_OCTETS, 2 * S5_OCT_STATE, LANES), full3),
            pl.BlockSpec((S5_QUADS, SUBLANES, LANES), full3),
            pl.BlockSpec((S5_QUADS, SUBLANES, LANES), full3),
            pl.BlockSpec((1, S5_WIDTH), full2),
            pl.BlockSpec((S5_WIDTH, S5_WIDTH), full2),
            pl.BlockSpec((1, S5_WIDTH), full2),
        ],
        out_specs=pl.BlockSpec((tb, S5_WIDTH), lambda b, i: (b * nblk + i, 0)),
        out_shape=jax.ShapeDtypeStruct((nt, S5_WIDTH), BF16),
        scratch_shapes=[pltpu.VMEM((S5_BLOCKS * tpad, LANES), F32),
                        pltpu.VMEM((2 * S5_QUADS, SUBLANES, LANES), F32)],
        compiler_params=pltpu.CompilerParams(
            dimension_semantics=("parallel", "arbitrary"), vmem_limit_bytes=VMEM_LIMIT),
        name="s5_branch",
    )(proj_b, proj_b, bb, co, lr, li, d_s5.reshape(1, -1), w_glu.astype(BF16), b_glu.reshape(1, -1))


def _dot3(a, b):
    a_hi = a.astype(BF16)
    a_lo = (a - a_hi.astype(F32)).astype(BF16)
    b_hi = b.astype(BF16)
    b_lo = (b - b_hi.astype(F32)).astype(BF16)
    return _dot(a_hi, b_hi) + _dot(a_hi, b_lo) + _dot(a_lo, b_hi)


def _rwkv_kernel(r_ref, k_ref, v_ref, z_ref, lo_ref, mu_ref, mulo_ref, wl_ref, w0_ref, a0_ref, kk_ref, ka_ref,
                 rk_ref, lw_ref, lb_ref, ones_ref, tri_ref, y_ref, st_ref, carry_ref, *, tb):
    @pl.when(pl.program_id(2) == 0)
    def _():
        st_ref[...] = jnp.zeros_like(st_ref)
        carry_ref[...] = jnp.zeros_like(carry_ref)

    L = RWKV_CHUNK
    hd = RWKV_HEAD_DIM
    row8 = lax.broadcasted_iota(jnp.int32, (SUBLANES, 1), 0)
    lane = lax.broadcasted_iota(jnp.int32, (1, LANES), 1)
    head_mask = (lane < hd, lane >= hd)

    def lerp(ref, idx, mu):
        cur = ref[...].astype(F32)
        prev = _shift_rows(cur, carry_ref[idx], 1, row8)
        carry_ref[idx] = cur[tb - SUBLANES:tb]
        return cur + (prev - cur) * mu

    r = lerp(r_ref, 0, mu_ref[0:1, :])
    k = lerp(k_ref, 1, mu_ref[1:2, :])
    v = lerp(v_ref, 2, mu_ref[2:3, :])
    zc = lerp(z_ref, 3, mu_ref[3:4, :])
    xl = lerp(lo_ref, 4, mulo_ref[...])
    lin = jnp.where(head_mask[0], jnp.tanh(xl), xl).astype(BF16)
    wa = _dot(lin, wl_ref[...])
    w_log = -_softplus(-(w0_ref[...] + wa[:, :LANES])) - 0.5
    logdec = -jnp.exp(w_log)
    a = _sigmoid(a0_ref[...] + wa[:, LANES:])
    ones = ones_ref[...]
    kkr = k * kk_ref[...]
    kk = kkr * lax.rsqrt(jnp.maximum(_dot_hilo(kkr * kkr, ones), 1e-24))
    k2 = k * (1.0 + (a - 1.0) * ka_ref[...])
    av = -kk
    bv = kk * a

    ti = lax.broadcasted_iota(jnp.int32, (L, LANES), 0)
    si = lax.broadcasted_iota(jnp.int32, (L, LANES), 1) % L
    strict = si < ti
    incl = si <= ti
    ri = lax.broadcasted_iota(jnp.int32, (LANES, LANES), 0)
    cj = lax.broadcasted_iota(jnp.int32, (LANES, LANES), 1)
    blockdiag = (ri // hd) == (cj // hd)
    eye = ri == cj
    tri = tri_ref[...]
    zeros_l = jnp.zeros((L, LANES), F32)
    state = st_ref[...]

    ys = []
    for ci in range(tb // L):
        sl = slice(ci * L, (ci + 1) * L)
        ld = logdec[sl]
        cum = jnp.dot(tri, ld, precision=HIGHEST, preferred_element_type=F32)
        cl = cum[L - 1:L, :]
        e_pos = jnp.exp(cum)
        e_neg = jnp.exp(-cum)
        e_end = jnp.exp(cl - cum)
        rt = r[sl] * e_pos
        at = av[sl] * jnp.exp(cum - ld)
        kt = k2[sl] * e_neg
        bt = bv[sl] * e_neg
        kh = k2[sl] * e_end
        bh = bv[sl] * e_end
        vc = v[sl]
        rhs1 = jnp.concatenate([bt, kt], axis=0).astype(BF16)
        w_sum = None
        u_sum = None
        ar_rows = []
        for h in range(2):
            m = head_mask[h]
            at_h = jnp.where(m, at, 0.0)
            lhs = jnp.concatenate([at_h, jnp.where(m, rt, 0.0)], axis=0).astype(BF16)
            a4 = _dot_nt(lhs, rhs1)
            aa = jnp.where(strict, a4[0:L], 0.0)
            ar_rows.append(jnp.where(incl, a4[L:2 * L], 0.0))
            v_h = jnp.where(m, vc, 0.0)
            akv = _dot(aa.astype(BF16), jnp.concatenate([zeros_l, v_h], axis=0).astype(BF16))
            x = jnp.concatenate([at_h, akv], axis=1)
            p = aa[:, 0:L]
            n_dbl = L.bit_length() - 1
            for i in range(n_dbl):
                p_bf = p.astype(BF16)
                x = x + _dot(p_bf, x.astype(BF16))
                if i + 1 < n_dbl:
                    p = _dot(p_bf, p_bf)
            w_sum = x[:, :LANES] if w_sum is None else w_sum + x[:, :LANES]
            u_sum = x[:, LANES:] if u_sum is None else u_sum + x[:, LANES:]
        rhs2 = jnp.concatenate([jnp.concatenate([w_sum, u_sum], axis=1),
                                jnp.concatenate([zeros_l, vc], axis=1)], axis=0).astype(BF16)
        qy = _dot(jnp.concatenate(ar_rows, axis=0).astype(BF16), rhs2)
        qy = jnp.where(jnp.concatenate([head_mask[0], head_mask[0]], axis=1), qy[0:L], qy[L:2 * L])
        q_mat = rt + qy[:, :LANES]
        y0 = qy[:, LANES:]
        lhs3 = jnp.concatenate([bh, kh], axis=0).T.astype(BF16)
        mn = jnp.where(jnp.concatenate([blockdiag, blockdiag], axis=1), _dot(lhs3, rhs2), 0.0)
        ys.append(_dot(q_mat.astype(BF16), state.astype(BF16)) + y0)
        gcol = jnp.sum(jnp.where(eye, jnp.exp(cl), 0.0), axis=1, keepdims=True)
        state = gcol * state + _dot3(mn[:, :LANES], state) + mn[:, LANES:]
    st_ref[...] = state

    y = jnp.concatenate(ys, axis=0)
    inv_hd = 1.0 / hd
    mean = _dot_hilo(y, ones) * inv_hd
    yc = y - mean
    var = _dot_hilo(yc * yc, ones) * inv_hd
    yn = yc * lax.rsqrt(var + RWKV_LNX_EPS) * lw_ref[...] + lb_ref[...]
    rk = _dot_hilo(r * k2 * rk_ref[...], ones)
    y_ref[...] = ((yn + rk * v) * _silu(zc)).astype(y_ref.dtype)


def _rwkv_branch(proj_c, batch, seq, mu, mu_lora, w0, w2, a0, a2, k_k, k_a, r_k, lnx_w, lnx_b, tb):
    nt = batch * seq
    nblk = seq // tb
    npair = RWKV_WIDTH // LANES
    cb = RWKV_WIDTH // LANES
    w2p = w2.reshape(RWKV_LORA, npair, LANES).transpose(1, 0, 2)
    a2p = a2.reshape(RWKV_LORA, npair, LANES).transpose(1, 0, 2)
    zer = jnp.zeros_like(w2p)
    wl = jnp.concatenate([jnp.concatenate([w2p, zer], axis=2), jnp.concatenate([zer, a2p], axis=2)], axis=1)
    wl = wl.reshape(npair * LANES, 2 * LANES).astype(BF16)
    ones = (jnp.arange(LANES)[:, None] // RWKV_HEAD_DIM == jnp.arange(LANES)[None, :] // RWKV_HEAD_DIM).astype(BF16)
    tri = jnp.tril(jnp.ones((RWKV_CHUNK, RWKV_CHUNK), F32))
    vec = lambda p: p.reshape(1, RWKV_WIDTH)
    tok = lambda off: pl.BlockSpec((tb, LANES), lambda b, h, i: (b * nblk + i, off + h))
    pvec = pl.BlockSpec((1, LANES), lambda b, h, i: (0, h))
    full = lambda b, h, i: (0, 0)
    return pl.pallas_call(
        functools.partial(_rwkv_kernel, tb=tb),
        grid=(batch, npair, nblk),
        in_specs=[
            tok(0), tok(cb), tok(2 * cb), tok(3 * cb),
            pl.BlockSpec((tb, LANES), lambda b, h, i: (b * nblk + i, 4 * cb)),
            pl.BlockSpec((4, LANES), lambda b, h, i: (0, h)),
            pl.BlockSpec((1, LANES), full),
            pl.BlockSpec((LANES, 2 * LANES), lambda b, h, i: (h, 0)),
            pvec, pvec, pvec, pvec, pvec, pvec, pvec,
            pl.BlockSpec((LANES, LANES), full),
            pl.BlockSpec((RWKV_CHUNK, RWKV_CHUNK), full),
        ],
        out_specs=pl.BlockSpec((tb, LANES), lambda b, h, i: (b * nblk + i, h)),
        out_shape=jax.ShapeDtypeStruct((nt, RWKV_WIDTH), BF16),
        scratch_shapes=[pltpu.VMEM((LANES, LANES), F32),
                        pltpu.VMEM((5, SUBLANES, LANES), F32)],
        compiler_params=pltpu.CompilerParams(
            dimension_semantics=("parallel", "parallel", "arbitrary"), vmem_limit_bytes=VMEM_LIMIT),
        name="rwkv_branch",
    )(proj_c, proj_c, proj_c, proj_c, proj_c, mu, mu_lora.reshape(1, 2 * RWKV_LORA), wl, vec(w0), vec(a0),
      vec(k_k), vec(k_a), vec(r_k), vec(lnx_w), vec(lnx_b), ones, tri)


def _merge_kernel(x_ref, ya_ref, yb_ref, yc_ref, wg0_ref, wg1_ref, wg2_ref, wa_ref, wb_ref, wc_ref, gb_ref, o_ref):
    x = x_ref[...]
    m = _sigmoid(_dot(x, wg0_ref[...]) + gb_ref[0:1, :]) * _dot(ya_ref[...], wa_ref[...])
    m = m + _sigmoid(_dot(x, wg1_ref[...]) + gb_ref[1:2, :]) * _dot(yb_ref[...], wb_ref[...])
    m = m + _sigmoid(_dot(x, wg2_ref[...]) + gb_ref[2:3, :]) * _dot(yc_ref[...], wc_ref[...])
    o_ref[...] = m.astype(o_ref.dtype)


def _merge(x_bf, ya, yb, yc, wg, gate_b, wa, wb, wc, tm, tn):
    nt, d = x_bf.shape
    nj = d // tn
    act = lambda w: pl.BlockSpec((tm, w), lambda i, j: (i, 0))
    wcol = lambda k, off: pl.BlockSpec((k, tn), lambda i, j: (0, off + j))
    return pl.pallas_call(
        _merge_kernel,
        grid=(nt // tm, nj),
        in_specs=[act(d), act(SSD_WIDTH), act(S5_WIDTH), act(RWKV_WIDTH),
                  wcol(d, 0), wcol(d, nj), wcol(d, 2 * nj),
                  wcol(SSD_WIDTH, 0), wcol(S5_WIDTH, 0), wcol(RWKV_WIDTH, 0),
                  pl.BlockSpec((N_BRANCHES, tn), lambda i, j: (0, j))],
        out_specs=pl.BlockSpec((tm, tn), lambda i, j: (i, j)),
        out_shape=jax.ShapeDtypeStruct((nt, d), BF16),
        compiler_params=pltpu.CompilerParams(
            dimension_semantics=("parallel", "arbitrary"), vmem_limit_bytes=VMEM_LIMIT),
        name="gated_merge",
    )(x_bf, ya, yb, yc, wg, wg, wg, wa, wb, wc, gate_b)


def _out_kernel(m_ref, x_ref, wo_ref, g_ref, b_ref, xo_ref, xb_ref):
    h = DEEPNORM_ALPHA * x_ref[...] + _dot(m_ref[...], wo_ref[...])
    mu = jnp.mean(h, axis=-1, keepdims=True)
    hc = h - mu
    var = jnp.mean(hc * hc, axis=-1, keepdims=True)
    y = hc * lax.rsqrt(var + LN_EPS) * g_ref[...] + b_ref[...]
    xo_ref[...] = y
    xb_ref[...] = y.astype(BF16)


def _out_norm(merged, x_f32, wo, ln_g, ln_b, tm):
    nt, d = x_f32.shape
    row = pl.BlockSpec((tm, d), lambda i: (i, 0))
    vec = pl.BlockSpec((1, d), lambda i: (0, 0))
    return pl.pallas_call(
        _out_kernel,
        grid=(nt // tm,),
        in_specs=[row, row, pl.BlockSpec((d, d), lambda i: (0, 0)), vec, vec],
        out_specs=[row, row],
        out_shape=[jax.ShapeDtypeStruct((nt, d), F32), jax.ShapeDtypeStruct((nt, d), BF16)],
        compiler_params=pltpu.CompilerParams(
            dimension_semantics=("parallel",), vmem_limit_bytes=VMEM_LIMIT),
        name="out_norm",
    )(merged, x_f32, wo, ln_g.reshape(1, d), ln_b.reshape(1, d))


def kernel(x, w_in, ssd_conv_w, ssd_conv_b, ssd_dt_bias, ssd_a_log, ssd_d, ssd_norm_w, s5_lambda_re, s5_lambda_im, s5_log_dt, s5_b_re, s5_b_im, s5_c_re, s5_c_im, s5_d, s5_w_glu, s5_b_glu, rwkv_mu, rwkv_mu_lora, rwkv_w0, rwkv_w2, rwkv_a0, rwkv_a2, rwkv_k_k, rwkv_k_a, rwkv_r_k, rwkv_lnx_w, rwkv_lnx_b, gate_b, w_branch_a, w_branch_b, w_branch_c, w_out, ln_g, ln_b):
    b, s, d = x.shape
    nt = b * s
    tm = min(1024, nt)
    tb = min(256, s)
    xf = x.reshape(nt, d)
    xb = xf.astype(BF16)
    for l in range(DEPTH):
        wl = w_in[l]
        w_a = jnp.pad(wl[:, OFF_A:OFF_A + LEN_A], ((0, 0), (0, PAD_A))).astype(BF16)
        w_b = wl[:, OFF_B:OFF_B + LEN_B].astype(BF16)
        w_c = wl[:, OFF_C:OFF_C + LEN_C].astype(BF16)
        w_g = wl[:, OFF_G:OFF_G + LEN_G].astype(BF16)
        pa = _project(xb, w_a, tm, (LEN_A + PAD_A) // 7)
        pb = _project(xb, w_b, tm, LEN_B // 2)
        pc = _project(xb, w_c, tm, LEN_C // 3)
        ya = _ssd_branch(pa, b, s, ssd_conv_w[l], ssd_conv_b[l], ssd_dt_bias[l], ssd_a_log[l], ssd_d[l],
                         ssd_norm_w[l], tb)
        yb = _s5_branch(pb, b, s, s5_lambda_re[l], s5_lambda_im[l], s5_log_dt[l], s5_b_re[l], s5_b_im[l],
                        s5_c_re[l], s5_c_im[l], s5_d[l], s5_w_glu[l], s5_b_glu[l], tb)
        yc = _rwkv_branch(pc, b, s, rwkv_mu[l], rwkv_mu_lora[l], rwkv_w0[l], rwkv_w2[l], rwkv_a0[l], rwkv_a2[l],
                          rwkv_k_k[l], rwkv_k_a[l], rwkv_r_k[l], rwkv_lnx_w[l], rwkv_lnx_b[l], tb)
        merged = _merge(xb, ya, yb, yc, w_g, gate_b[l], w_branch_a[l].astype(BF16), w_branch_b[l].astype(BF16),
                        w_branch_c[l].astype(BF16), tm, 256)
        xf, xb = _out_norm(merged, xf, w_out[l].astype(BF16), ln_g[l], ln_b[l], min(512, nt))
    return xf.reshape(b, s, d)
```

```python
import functools

import jax
import jax.numpy as jnp
import numpy as np
from jax import lax
from jax.experimental import pallas as pl
from jax.experimental.pallas import tpu as pltpu

F32 = jnp.float32
BF16 = jnp.bfloat16
HIGHEST = lax.Precision.HIGHEST

LANES = 128
SUBLANES = 8
VMEM_LIMIT = 56 * 1024 * 1024

D_MODEL = 2048
DEPTH = 4
SSD_HEADS = 32
SSD_HEAD_DIM = 64
SSD_WIDTH = SSD_HEADS * SSD_HEAD_DIM
SSD_GROUPS = 8
SSD_STATE = 128
SSD_CONV = 4
SSD_CHUNK = 128
SSD_NORM_EPS = 1e-5
SSD_GROUP_W = SSD_WIDTH // SSD_GROUPS
S5_GROUP_CH = 16
S5_GROUPS = 64
S5_WIDTH = S5_GROUPS * S5_GROUP_CH
S5_STATE = 64
S5_OCTETS = S5_WIDTH // LANES
S5_OCT_STATE = (LANES // S5_GROUP_CH) * S5_STATE
RWKV_HEADS = 16
RWKV_HEAD_DIM = 64
RWKV_WIDTH = RWKV_HEADS * RWKV_HEAD_DIM
RWKV_LORA = 64
RWKV_LNX_EPS = 64e-5
RWKV_CHUNK = 64
RWKV_SUPER = 256
N_BRANCHES = 3
DEEPNORM_ALPHA = (2 * DEPTH) ** 0.25
LN_EPS = 1e-5

OFF_A = 0
LEN_A = SSD_WIDTH + (SSD_WIDTH + 2 * SSD_GROUPS * SSD_STATE) + SSD_HEADS
OFF_B = OFF_A + LEN_A
LEN_B = 2 * S5_WIDTH
OFF_C = OFF_B + LEN_B
LEN_C = 4 * RWKV_WIDTH + 2 * RWKV_LORA
OFF_G = OFF_C + LEN_C
LEN_G = N_BRANCHES * D_MODEL
PAD_A = (-LEN_A) % LANES


def _silu(x):
    return x / (1.0 + jnp.exp(-x))


def _sigmoid(x):
    return 1.0 / (1.0 + jnp.exp(-x))


def _softplus(x):
    return jnp.maximum(x, 0.0) + jnp.log(1.0 + jnp.exp(-jnp.abs(x)))


def _dot(a, b):
    return jnp.dot(a, b, preferred_element_type=F32)


def _dot_nt(a, b):
    return lax.dot_general(a, b, (((1,), (1,)), ((), ())), preferred_element_type=F32)


def _dot_hilo(a, b_bf):
    hi = a.astype(BF16)
    lo = (a - hi.astype(F32)).astype(BF16)
    return _dot(hi, b_bf) + _dot(lo, b_bf)


def _shift_rows(x, carry8, d, row8):
    sh = pltpu.roll(x, d, 0)
    head = jnp.where(row8 < d, pltpu.roll(carry8, d, 0), sh[0:SUBLANES])
    return jnp.concatenate([head, sh[SUBLANES:]], axis=0)


def _mm_kernel(x_ref, w_ref, o_ref):
    o_ref[...] = _dot(x_ref[...], w_ref[...]).astype(o_ref.dtype)


def _project(x_bf, w_bf, tm, tn):
    nt, d = x_bf.shape
    n = w_bf.shape[1]
    return pl.pallas_call(
        _mm_kernel,
        grid=(nt // tm, n // tn),
        in_specs=[pl.BlockSpec((tm, d), lambda i, j: (i, 0)),
                  pl.BlockSpec((d, tn), lambda i, j: (0, j))],
        out_specs=pl.BlockSpec((tm, tn), lambda i, j: (i, j)),
        out_shape=jax.ShapeDtypeStruct((nt, n), BF16),
        compiler_params=pltpu.CompilerParams(
            dimension_semantics=("parallel", "arbitrary"), vmem_limit_bytes=VMEM_LIMIT),
        name="in_proj",
    )(x_bf, w_bf)


def _ssd_kernel(z_ref, xs_ref, bc_ref, dt_ref, cwx_ref, cbx_ref, cwbc_ref, cbbc_ref,
                dtb_ref, a_ref, dsk_ref, nw_ref, exp_ref, tri_ref,
                y_ref, state_ref, cx_ref, cbc_ref, *, tb):
    @pl.when(pl.program_id(1) == 0)
    def _():
        state_ref[...] = jnp.zeros_like(state_ref)
        cx_ref[...] = jnp.zeros_like(cx_ref)
        cbc_ref[...] = jnp.zeros_like(cbc_ref)

    row8 = lax.broadcasted_iota(jnp.int32, (SUBLANES, 1), 0)

    def conv(u_ref, carry_ref, w_ref, b_ref):
        u = u_ref[...].astype(F32)
        c8 = carry_ref[...]
        acc = u * w_ref[SSD_CONV - 1:SSD_CONV, :] + b_ref[...]
        for d in range(1, SSD_CONV):
            acc = acc + _shift_rows(u, c8, d, row8) * w_ref[SSD_CONV - 1 - d:SSD_CONV - d, :]
        carry_ref[...] = u[tb - SUBLANES:tb]
        return _silu(acc)

    xs = conv(xs_ref, cx_ref, cwx_ref, cbx_ref)
    bc = conv(bc_ref, cbc_ref, cwbc_ref, cbbc_ref)

    L = SSD_CHUNK
    ti = lax.broadcasted_iota(jnp.int32, (L, L), 0)
    si = lax.broadcasted_iota(jnp.int32, (L, L), 1)
    causal = ti >= si
    lane = lax.broadcasted_iota(jnp.int32, (1, LANES), 1)
    lo_half = lane < SSD_HEAD_DIM
    tri = tri_ref[...]
    expand = exp_ref[...]
    gsn = SSD_GROUPS * SSD_STATE

    for ci in range(tb // L):
        r0 = ci * L
        dt = _softplus(dt_ref[r0:r0 + L, :].astype(F32) + dtb_ref[...])
        da = dt * a_ref[...]
        cs = jnp.dot(tri, da, precision=HIGHEST, preferred_element_type=F32)
        cs_t = cs.T
        dt_t = dt.T
        e_exp = _dot_hilo(jnp.exp(cs), expand)
        for g in range(SSD_GROUPS):
            c0 = g * SSD_GROUP_W
            x_g = xs[r0:r0 + L, c0:c0 + SSD_GROUP_W]
            b_g = bc[r0:r0 + L, g * SSD_STATE:(g + 1) * SSD_STATE]
            c_g = bc[r0:r0 + L, gsn + g * SSD_STATE:gsn + (g + 1) * SSD_STATE]
            c_bf = c_g.astype(BF16)
            cb = _dot_nt(c_bf, b_g.astype(BF16))
            b_t = b_g.T
            st_old = state_ref[:, c0:c0 + SSD_GROUP_W]
            y_off = _dot(c_bf, st_old.astype(BF16))
            y_parts, new_parts = [], []
            for pr in range(2):
                xp = x_g[:, pr * LANES:(pr + 1) * LANES]
                x_half = (jnp.where(lo_half, xp, 0.0).astype(BF16), jnp.where(lo_half, 0.0, xp).astype(BF16))
                y_p = None
                n_p = None
                for k in range(2):
                    h = g * 4 + pr * 2 + k
                    col = cs[:, h:h + 1]
                    row = cs_t[h:h + 1, :]
                    dtrow = dt_t[h:h + 1, :]
                    lmat = jnp.exp(jnp.where(causal, col - row, -1e30))
                    m_h = (cb * lmat * dtrow).astype(BF16)
                    y_k = _dot(m_h, x_half[k])
                    wrow = jnp.exp(cs_t[h:h + 1, L - 1:L] - row) * dtrow
                    n_k = _dot((b_t * wrow).astype(BF16), x_half[k])
                    y_p = y_k if y_p is None else y_p + y_k
                    n_p = n_k if n_p is None else n_p + n_k
                y_parts.append(y_p)
                new_parts.append(n_p)
            y_diag = jnp.concatenate(y_parts, axis=1)
            new_g = jnp.concatenate(new_parts, axis=1)
            e_g = e_exp[:, c0:c0 + SSD_GROUP_W]
            state_ref[:, c0:c0 + SSD_GROUP_W] = st_old * e_g[L - 1:L, :] + new_g
            y = y_diag + y_off * e_g + dsk_ref[:, c0:c0 + SSD_GROUP_W] * x_g
            y = y * _silu(z_ref[r0:r0 + L, c0:c0 + SSD_GROUP_W].astype(F32))
            ms = jnp.mean(y * y, axis=-1, keepdims=True)
            y = y * lax.rsqrt(ms + SSD_NORM_EPS) * nw_ref[:, c0:c0 + SSD_GROUP_W]
            y_ref[r0:r0 + L, c0:c0 + SSD_GROUP_W] = y.astype(y_ref.dtype)


def _ssd_branch(proj_a, batch, seq, conv_w, conv_b, dt_bias, a_log, d_skip, norm_w, tb):
    nt = batch * seq
    nblk = seq // tb
    w2 = SSD_WIDTH // D_MODEL
    assert w2 == 1
    dt_blk = (2 * SSD_WIDTH + 2 * SSD_GROUPS * SSD_STATE) // LANES
    pad_h = LANES - SSD_HEADS
    dtb = jnp.pad(dt_bias, (0, pad_h)).reshape(1, LANES)
    a_neg = jnp.pad(-jnp.exp(a_log.astype(F32)), (0, pad_h)).reshape(1, LANES)
    dsk = jnp.repeat(d_skip, SSD_HEAD_DIM).reshape(1, SSD_WIDTH)
    expand = (jnp.arange(LANES)[:, None] == (jnp.arange(SSD_WIDTH)[None, :] // SSD_HEAD_DIM)).astype(BF16)
    tri = jnp.tril(jnp.ones((SSD_CHUNK, SSD_CHUNK), F32))
    row = lambda b, i: (b * nblk + i, 0)
    full = lambda b, i: (0, 0)
    return pl.pallas_call(
        functools.partial(_ssd_kernel, tb=tb),
        grid=(batch, nblk),
        in_specs=[
            pl.BlockSpec((tb, SSD_WIDTH), lambda b, i: (b * nblk + i, 0)),
            pl.BlockSpec((tb, SSD_WIDTH), lambda b, i: (b * nblk + i, 1)),
            pl.BlockSpec((tb, SSD_WIDTH), lambda b, i: (b * nblk + i, 2)),
            pl.BlockSpec((tb, LANES), lambda b, i: (b * nblk + i, dt_blk)),
            pl.BlockSpec((SSD_CONV, SSD_WIDTH), lambda b, i: (0, 0)),
            pl.BlockSpec((1, SSD_WIDTH), lambda b, i: (0, 0)),
            pl.BlockSpec((SSD_CONV, SSD_WIDTH), lambda b, i: (0, 1)),
            pl.BlockSpec((1, SSD_WIDTH), lambda b, i: (0, 1)),
            pl.BlockSpec((1, LANES), full), pl.BlockSpec((1, LANES), full),
            pl.BlockSpec((1, SSD_WIDTH), full), pl.BlockSpec((1, SSD_WIDTH), full),
            pl.BlockSpec((LANES, SSD_WIDTH), full), pl.BlockSpec((SSD_CHUNK, SSD_CHUNK), full),
        ],
        out_specs=pl.BlockSpec((tb, SSD_WIDTH), row),
        out_shape=jax.ShapeDtypeStruct((nt, SSD_WIDTH), BF16),
        scratch_shapes=[pltpu.VMEM((SSD_STATE, SSD_WIDTH), F32),
                        pltpu.VMEM((SUBLANES, SSD_WIDTH), F32),
                        pltpu.VMEM((SUBLANES, SSD_WIDTH), F32)],
        compiler_params=pltpu.CompilerParams(
            dimension_semantics=("parallel", "arbitrary"), vmem_limit_bytes=VMEM_LIMIT),
        name="ssd_branch",
    )(proj_a, proj_a, proj_a, proj_a, conv_w, conv_b.reshape(1, -1), conv_w, conv_b.reshape(1, -1),
      dtb, a_neg, dsk, norm_w.reshape(1, -1), expand, tri)


S5_QUADS = 4
S5_BLOCKS = 2 * S5_GROUPS * S5_STATE // LANES
S5_QBLK = S5_BLOCKS // S5_QUADS


def _s5_kernel(u_ref, z_ref, bb_ref, co_ref, lr_ref, li_ref, d_ref, wglu_ref, bglu_ref,
               y_ref, scr_ref, st_ref, *, tb, tpad):
    @pl.when(pl.program_id(1) == 0)
    def _():
        st_ref[...] = jnp.zeros_like(st_ref)

    half_blk = S5_OCT_STATE // LANES
    u_bf = u_ref[...]
    for o in range(S5_OCTETS):
        bu = _dot(u_bf[:, o * LANES:(o + 1) * LANES], bb_ref[o])
        q, half = divmod(o, 2)
        for j in range(half_blk):
            jr = S5_QBLK * q + half_blk * half + j
            ji = jr + SUBLANES
            scr_ref[jr * tpad:jr * tpad + tb, :] = bu[:, j * LANES:(j + 1) * LANES]
            scr_ref[ji * tpad:ji * tpad + tb, :] = bu[:, S5_OCT_STATE + j * LANES:S5_OCT_STATE + (j + 1) * LANES]

    lr = [lr_ref[q] for q in range(S5_QUADS)]
    li = [li_ref[q] for q in range(S5_QUADS)]

    def step(t, carry):
        new = []
        for q in range(S5_QUADS):
            sr, si = carry[2 * q], carry[2 * q + 1]
            re_rows = pl.ds(S5_QBLK * q * tpad + t, SUBLANES, stride=tpad)
            im_rows = pl.ds((S5_QBLK * q + SUBLANES) * tpad + t, SUBLANES, stride=tpad)
            nr = lr[q] * sr - li[q] * si + scr_ref[re_rows, :]
            ni = lr[q] * si + li[q] * sr + scr_ref[im_rows, :]
            scr_ref[re_rows, :] = nr
            scr_ref[im_rows, :] = ni
            new += [nr, ni]
        return tuple(new)

    carry = lax.fori_loop(0, tb, step, tuple(st_ref[i] for i in range(2 * S5_QUADS)), unroll=8)
    for i in range(2 * S5_QUADS):
        st_ref[i] = carry[i]

    ys = []
    for o in range(S5_OCTETS):
        q, half = divmod(o, 2)
        blks = []
        for part in range(2):
            for j in range(half_blk):
                jb = S5_QBLK * q + SUBLANES * part + half_blk * half + j
                blks.append(scr_ref[jb * tpad:jb * tpad + tb, :].astype(BF16))
        ys.append(_dot(jnp.concatenate(blks, axis=1), co_ref[o]))
    y = jnp.concatenate(ys, axis=1) + d_ref[...] * u_bf.astype(F32)
    y = 0.5 * y * (1.0 + jnp.tanh(0.7978845608028654 * (y + 0.044715 * (y * y * y))))
    gl = _dot(y.astype(BF16), wglu_ref[...]) + bglu_ref[...]
    y = y * _sigmoid(gl)
    y_ref[...] = (y * _silu(z_ref[...].astype(F32))).astype(y_ref.dtype)


def _s5_branch(proj_b, batch, seq, lam_re, lam_im, log_dt, b_re, b_im, c_re, c_im, d_s5, w_glu, b_glu, tb):
    nt = batch * seq
    nblk = seq // tb
    tpad = tb + SUBLANES
    dt = jnp.exp(log_dt)[:, None]
    mag = jnp.exp(lam_re * dt)
    ang = lam_im * dt
    lb_re, lb_im = mag * jnp.cos(ang), mag * jnp.sin(ang)
    den = jnp.square(lam_re) + jnp.square(lam_im)
    nr, ni = lb_re - 1.0, lb_im
    q_re = (nr * lam_re + ni * lam_im) / den
    q_im = (ni * lam_re - nr * lam_im) / den
    bb_re = q_re[..., None] * b_re - q_im[..., None] * b_im
    bb_im = q_re[..., None] * b_im + q_im[..., None] * b_re
    g8 = LANES // S5_GROUP_CH
    eye = jnp.eye(g8, dtype=F32)

    def in_mat(bb):
        m = jnp.einsum('ogph,gk->oghkp', bb.reshape(S5_OCTETS, g8, S5_STATE, S5_GROUP_CH), eye)
        return m.reshape(S5_OCTETS, LANES, S5_OCT_STATE)

    def out_mat(c):
        m = jnp.einsum('oghp,gk->okpgh', c.reshape(S5_OCTETS, g8, S5_GROUP_CH, S5_STATE), eye)
        return m.reshape(S5_OCTETS, S5_OCT_STATE, LANES)

    bb = jnp.concatenate([in_mat(bb_re), in_mat(bb_im)], axis=2).astype(BF16)
    co = jnp.concatenate([out_mat(c_re), -out_mat(c_im)], axis=1).astype(BF16)
    lr = lb_re.reshape(S5_QUADS, SUBLANES, LANES)
    li = lb_im.reshape(S5_QUADS, SUBLANES, LANES)
    full2 = lambda b, i: (0, 0)
    full3 = lambda b, i: (0, 0, 0)
    return pl.pallas_call(
        functools.partial(_s5_kernel, tb=tb, tpad=tpad),
        grid=(batch, nblk),
        in_specs=[
            pl.BlockSpec((tb, S5_WIDTH), lambda b, i: (b * nblk + i, 0)),
            pl.BlockSpec((tb, S5_WIDTH), lambda b, i: (b * nblk + i, 1)),
            pl.BlockSpec((S5_OCTETS, LANES, 2 * S5_OCT_STATE), full3),
            pl.BlockSpec((S5_OCTETS, 2 * S5_OCT_STATE, LANES), full3),
            pl.BlockSpec((S5_QUADS, SUBLANES, LANES), full3),
            pl.BlockSpec((S5_QUADS, SUBLANES, LANES), full3),
            pl.BlockSpec((1, S5_WIDTH), full2),
            pl.BlockSpec((S5_WIDTH, S5_WIDTH), full2),
            pl.BlockSpec((1, S5_WIDTH), full2),
        ],
        out_specs=pl.BlockSpec((tb, S5_WIDTH), lambda b, i: (b * nblk + i, 0)),
        out_shape=jax.ShapeDtypeStruct((nt, S5_WIDTH), BF16),
        scratch_shapes=[pltpu.VMEM((S5_BLOCKS * tpad, LANES), F32),
                        pltpu.VMEM((2 * S5_QUADS, SUBLANES, LANES), F32)],
        compiler_params=pltpu.CompilerParams(
            dimension_semantics=("parallel", "arbitrary"), vmem_limit_bytes=VMEM_LIMIT),
        name="s5_branch",
    )(proj_b, proj_b, bb, co, lr, li, d_s5.reshape(1, -1), w_glu.astype(BF16), b_glu.reshape(1, -1))


def _dot_x3(m_bf, x):
    hi = x.astype(BF16)
    r1 = x - hi.astype(F32)
    mid = r1.astype(BF16)
    lo = (r1 - mid.astype(F32)).astype(BF16)
    return _dot(m_bf, hi) + _dot(m_bf, mid) + _dot(m_bf, lo)


def _dot3(a, b):
    a_hi = a.astype(BF16)
    a_lo = (a - a_hi.astype(F32)).astype(BF16)
    b_hi = b.astype(BF16)
    b_lo = (b - b_hi.astype(F32)).astype(BF16)
    return _dot(a_hi, b_hi) + _dot(a_hi, b_lo) + _dot(a_lo, b_hi)


def _rwkv_kernel(r_ref, k_ref, v_ref, z_ref, lo_ref, mu_ref, mulo_ref, wl_ref, w0_ref, a0_ref, kk_ref, ka_ref,
                 rk_ref, lw_ref, lb_ref, ones_ref, tri_ref, cones_ref, y_ref, st_ref, carry_ref, *, tb):
    @pl.when(pl.program_id(2) == 0)
    def _():
        st_ref[...] = jnp.zeros_like(st_ref)
        carry_ref[...] = jnp.zeros_like(carry_ref)

    L = RWKV_CHUNK
    hd = RWKV_HEAD_DIM
    row8 = lax.broadcasted_iota(jnp.int32, (SUBLANES, 1), 0)
    lane = lax.broadcasted_iota(jnp.int32, (1, LANES), 1)
    head_mask = (lane < hd, lane >= hd)

    def lerp(ref, idx, mu):
        cur = ref[...].astype(F32)
        prev = _shift_rows(cur, carry_ref[idx], 1, row8)
        carry_ref[idx] = cur[tb - SUBLANES:tb]
        return cur + (prev - cur) * mu

    r = lerp(r_ref, 0, mu_ref[0:1, :])
    k = lerp(k_ref, 1, mu_ref[1:2, :])
    v = lerp(v_ref, 2, mu_ref[2:3, :])
    zc = lerp(z_ref, 3, mu_ref[3:4, :])
    xl = lerp(lo_ref, 4, mulo_ref[...])
    lin = jnp.where(head_mask[0], jnp.tanh(xl), xl).astype(BF16)
    wa = _dot(lin, wl_ref[...])
    w_log = -_softplus(-(w0_ref[...] + wa[:, :LANES])) - 0.5
    logdec = -jnp.exp(w_log)
    a = _sigmoid(a0_ref[...] + wa[:, LANES:])
    ones = ones_ref[...]
    kkr = k * kk_ref[...]
    kk = kkr * lax.rsqrt(jnp.maximum(_dot_hilo(kkr * kkr, ones), 1e-24))
    k2 = k * (1.0 + (a - 1.0) * ka_ref[...])
    av = -kk
    bv = kk * a

    sbr = RWKV_SUPER
    nsb = tb // sbr
    ti = lax.broadcasted_iota(jnp.int32, (sbr, sbr), 0)
    si = lax.broadcasted_iota(jnp.int32, (sbr, sbr), 1)
    same_chunk = (ti // L) == (si // L)
    strict = same_chunk & (si < ti)
    incl = same_chunk & (si <= ti)
    ri = lax.broadcasted_iota(jnp.int32, (LANES, LANES), 0)
    cj = lax.broadcasted_iota(jnp.int32, (LANES, LANES), 1)
    blockdiag2 = jnp.concatenate([(ri // hd) == (cj // hd)] * 2, axis=1)
    eye = ri == cj
    tri = tri_ref[...]
    cones = cones_ref[...]
    zeros_l = jnp.zeros((L, LANES), F32)

    pre = []
    for sb in range(nsb):
        sl = slice(sb * sbr, (sb + 1) * sbr)
        ld = logdec[sl]
        cum = _dot_x3(tri, ld)
        ctot = _dot_x3(cones, ld)
        e_neg = jnp.exp(-cum)
        e_end = jnp.exp(ctot - cum)
        pre.append(dict(rt=r[sl] * jnp.exp(cum), at=av[sl] * jnp.exp(cum - ld), kt=k2[sl] * e_neg,
                        bt=bv[sl] * e_neg, kh=k2[sl] * e_end, bh=bv[sl] * e_end, v=v[sl], ctot=ctot))

    chains = [(sb, h) for sb in range(nsb) for h in range(2)]
    xs, ps, arb, ark, vh = {}, {}, {}, {}, {}
    for c in chains:
        sb, h = c
        d = pre[sb]
        m = head_mask[h]
        at_h = jnp.where(m, d['at'], 0.0)
        lhs = jnp.concatenate([at_h, jnp.where(m, d['rt'], 0.0)], axis=0).astype(BF16)
        rhs1 = jnp.concatenate([d['bt'], d['kt']], axis=0).astype(BF16)
        a4 = _dot_nt(lhs, rhs1)
        ps[c] = jnp.where(strict, a4[0:sbr, 0:sbr], 0.0)
        aak = jnp.where(strict, a4[0:sbr, sbr:], 0.0).astype(BF16)
        arb[c] = jnp.where(incl, a4[sbr:, 0:sbr], 0.0).astype(BF16)
        ark[c] = jnp.where(incl, a4[sbr:, sbr:], 0.0).astype(BF16)
        vh[c] = jnp.where(m, d['v'], 0.0).astype(BF16)
        xs[c] = jnp.concatenate([at_h, _dot(aak, vh[c])], axis=1)
    n_dbl = L.bit_length() - 1
    for i in range(n_dbl):
        for c in chains:
            p_bf = ps[c].astype(BF16)
            xs[c] = xs[c] + _dot(p_bf, xs[c].astype(BF16))
            if i + 1 < n_dbl:
                ps[c] = _dot(p_bf, p_bf)
    w_sb, u_sb, q_sb, y0_sb = [], [], [], []
    for sb in range(nsb):
        w_s = xs[(sb, 0)][:, :LANES] + xs[(sb, 1)][:, :LANES]
        u_s = xs[(sb, 0)][:, LANES:] + xs[(sb, 1)][:, LANES:]
        q_s = pre[sb]['rt']
        y0_s = None
        for h in range(2):
            c = (sb, h)
            qy = _dot(arb[c], xs[c].astype(BF16))
            y0_h = qy[:, LANES:] + _dot(ark[c], vh[c])
            q_s = q_s + qy[:, :LANES]
            y0_s = y0_h if y0_s is None else y0_s + y0_h
        w_sb.append(w_s)
        u_sb.append(u_s)
        q_sb.append(q_s)
        y0_sb.append(y0_s)

    mns, gcols = [], []
    for ci in range(tb // L):
        sb, lo_r = divmod(ci * L, sbr)
        cs = slice(lo_r, lo_r + L)
        d = pre[sb]
        rhs2 = jnp.concatenate([jnp.concatenate([w_sb[sb][cs], u_sb[sb][cs]], axis=1),
                                jnp.concatenate([zeros_l, d['v'][cs]], axis=1)], axis=0).astype(BF16)
        lhs3 = jnp.concatenate([d['bh'][cs], d['kh'][cs]], axis=0).T.astype(BF16)
        mns.append(jnp.where(blockdiag2, _dot(lhs3, rhs2), 0.0))
        gl = jnp.exp(d['ctot'][lo_r:lo_r + 1, :])
        gcols.append(jnp.sum(jnp.where(eye, gl, 0.0), axis=1, keepdims=True))

    state = st_ref[...]
    ys = []
    for ci in range(tb // L):
        sb, lo_r = divmod(ci * L, sbr)
        cs = slice(lo_r, lo_r + L)
        ys.append(_dot(q_sb[sb][cs].astype(BF16), state.astype(BF16)) + y0_sb[sb][cs])
        state = gcols[ci] * state + _dot3(mns[ci][:, :LANES], state) + mns[ci][:, LANES:]
    st_ref[...] = state

    y = jnp.concatenate(ys, axis=0)
    inv_hd = 1.0 / hd
    mean = _dot_hilo(y, ones) * inv_hd
    yc = y - mean
    var = _dot_hilo(yc * yc, ones) * inv_hd
    yn = yc * lax.rsqrt(var + RWKV_LNX_EPS) * lw_ref[...] + lb_ref[...]
    rk = _dot_hilo(r * k2 * rk_ref[...], ones)
    y_ref[...] = ((yn + rk * v) * _silu(zc)).astype(y_ref.dtype)


def _rwkv_branch(proj_c, batch, seq, mu, mu_lora, w0, w2, a0, a2, k_k, k_a, r_k, lnx_w, lnx_b, tb):
    nt = batch * seq
    nblk = seq // tb
    npair = RWKV_WIDTH // LANES
    cb = RWKV_WIDTH // LANES
    w2p = w2.reshape(RWKV_LORA, npair, LANES).transpose(1, 0, 2)
    a2p = a2.reshape(RWKV_LORA, npair, LANES).transpose(1, 0, 2)
    zer = jnp.zeros_like(w2p)
    wl = jnp.concatenate([jnp.concatenate([w2p, zer], axis=2), jnp.concatenate([zer, a2p], axis=2)], axis=1)
    wl = wl.reshape(npair * LANES, 2 * LANES).astype(BF16)
    ones = (jnp.arange(LANES)[:, None] // RWKV_HEAD_DIM == jnp.arange(LANES)[None, :] // RWKV_HEAD_DIM).astype(BF16)
    idx = jnp.arange(RWKV_SUPER)
    same_chunk = idx[:, None] // RWKV_CHUNK == idx[None, :] // RWKV_CHUNK
    cones = same_chunk.astype(BF16)
    tri = (same_chunk & (idx[None, :] <= idx[:, None])).astype(BF16)
    vec = lambda p: p.reshape(1, RWKV_WIDTH)
    tok = lambda off: pl.BlockSpec((tb, LANES), lambda b, h, i: (b * nblk + i, off + h))
    pvec = pl.BlockSpec((1, LANES), lambda b, h, i: (0, h))
    full = lambda b, h, i: (0, 0)
    return pl.pallas_call(
        functools.partial(_rwkv_kernel, tb=tb),
        grid=(batch, npair, nblk),
        in_specs=[
            tok(0), tok(cb), tok(2 * cb), tok(3 * cb),
            pl.BlockSpec((tb, LANES), lambda b, h, i: (b * nblk + i, 4 * cb)),
            pl.BlockSpec((4, LANES), lambda b, h, i: (0, h)),
            pl.BlockSpec((1, LANES), full),
            pl.BlockSpec((LANES, 2 * LANES), lambda b, h, i: (h, 0)),
            pvec, pvec, pvec, pvec, pvec, pvec, pvec,
            pl.BlockSpec((LANES, LANES), full),
            pl.BlockSpec((RWKV_SUPER, RWKV_SUPER), full),
            pl.BlockSpec((RWKV_SUPER, RWKV_SUPER), full),
        ],
        out_specs=pl.BlockSpec((tb, LANES), lambda b, h, i: (b * nblk + i, h)),
        out_shape=jax.ShapeDtypeStruct((nt, RWKV_WIDTH), BF16),
        scratch_shapes=[pltpu.VMEM((LANES, LANES), F32),
                        pltpu.VMEM((5, SUBLANES, LANES), F32)],
        compiler_params=pltpu.CompilerParams(
            dimension_semantics=("parallel", "parallel", "arbitrary"), vmem_limit_bytes=VMEM_LIMIT),
        name="rwkv_branch",
    )(proj_c, proj_c, proj_c, proj_c, proj_c, mu, mu_lora.reshape(1, 2 * RWKV_LORA), wl, vec(w0), vec(a0),
      vec(k_k), vec(k_a), vec(r_k), vec(lnx_w), vec(lnx_b), ones, tri, cones)


def _merge_kernel(x_ref, ya_ref, yb_ref, yc_ref, wg0_ref, wg1_ref, wg2_ref, wa_ref, wb_ref, wc_ref, gb_ref, o_ref):
    x = x_ref[...]
    m = _sigmoid(_dot(x, wg0_ref[...]) + gb_ref[0:1, :]) * _dot(ya_ref[...], wa_ref[...])
    m = m + _sigmoid(_dot(x, wg1_ref[...]) + gb_ref[1:2, :]) * _dot(yb_ref[...], wb_ref[...])
    m = m + _sigmoid(_dot(x, wg2_ref[...]) + gb_ref[2:3, :]) * _dot(yc_ref[...], wc_ref[...])
    o_ref[...] = m.astype(o_ref.dtype)


def _merge(x_bf, ya, yb, yc, wg, gate_b, wa, wb, wc, tm, tn):
    nt, d = x_bf.shape
    nj = d // tn
    act = lambda w: pl.BlockSpec((tm, w), lambda i, j: (i, 0))
    wcol = lambda k, off: pl.BlockSpec((k, tn), lambda i, j: (0, off + j))
    return pl.pallas_call(
        _merge_kernel,
        grid=(nt // tm, nj),
        in_specs=[act(d), act(SSD_WIDTH), act(S5_WIDTH), act(RWKV_WIDTH),
                  wcol(d, 0), wcol(d, nj), wcol(d, 2 * nj),
                  wcol(SSD_WIDTH, 0), wcol(S5_WIDTH, 0), wcol(RWKV_WIDTH, 0),
                  pl.BlockSpec((N_BRANCHES, tn), lambda i, j: (0, j))],
        out_specs=pl.BlockSpec((tm, tn), lambda i, j: (i, j)),
        out_shape=jax.ShapeDtypeStruct((nt, d), BF16),
        compiler_params=pltpu.CompilerParams(
            dimension_semantics=("parallel", "arbitrary"), vmem_limit_bytes=VMEM_LIMIT),
        name="gated_merge",
    )(x_bf, ya, yb, yc, wg, wg, wg, wa, wb, wc, gate_b)


def _out_kernel(m_ref, x_ref, wo_ref, g_ref, b_ref, xo_ref, xb_ref):
    h = DEEPNORM_ALPHA * x_ref[...] + _dot(m_ref[...], wo_ref[...])
    mu = jnp.mean(h, axis=-1, keepdims=True)
    hc = h - mu
    var = jnp.mean(hc * hc, axis=-1, keepdims=True)
    y = hc * lax.rsqrt(var + LN_EPS) * g_ref[...] + b_ref[...]
    xo_ref[...] = y
    xb_ref[...] = y.astype(BF16)


def _out_norm(merged, x_f32, wo, ln_g, ln_b, tm):
    nt, d = x_f32.shape
    row = pl.BlockSpec((tm, d), lambda i: (i, 0))
    vec = pl.BlockSpec((1, d), lambda i: (0, 0))
    return pl.pallas_call(
        _out_kernel,
        grid=(nt // tm,),
        in_specs=[row, row, pl.BlockSpec((d, d), lambda i: (0, 0)), vec, vec],
        out_specs=[row, row],
        out_shape=[jax.ShapeDtypeStruct((nt, d), F32), jax.ShapeDtypeStruct((nt, d), BF16)],
        compiler_params=pltpu.CompilerParams(
            dimension_semantics=("parallel",), vmem_limit_bytes=VMEM_LIMIT),
        name="out_norm",
    )(merged, x_f32, wo, ln_g.reshape(1, d), ln_b.reshape(1, d))


def kernel(x, w_in, ssd_conv_w, ssd_conv_b, ssd_dt_bias, ssd_a_log, ssd_d, ssd_norm_w, s5_lambda_re, s5_lambda_im, s5_log_dt, s5_b_re, s5_b_im, s5_c_re, s5_c_im, s5_d, s5_w_glu, s5_b_glu, rwkv_mu, rwkv_mu_lora, rwkv_w0, rwkv_w2, rwkv_a0, rwkv_a2, rwkv_k_k, rwkv_k_a, rwkv_r_k, rwkv_lnx_w, rwkv_lnx_b, gate_b, w_branch_a, w_branch_b, w_branch_c, w_out, ln_g, ln_b):
    b, s, d = x.shape
    nt = b * s
    tm = min(1024, nt)
    tb = min(256, s)
    xf = x.reshape(nt, d)
    xb = xf.astype(BF16)
    w_in_bf = w_in.astype(BF16)
    for l in range(DEPTH):
        wl = w_in_bf[l]
        w_a = jnp.pad(wl[:, OFF_A:OFF_A + LEN_A], ((0, 0), (0, PAD_A)))
        w_b = wl[:, OFF_B:OFF_B + LEN_B]
        w_c = wl[:, OFF_C:OFF_C + LEN_C]
        w_g = wl[:, OFF_G:OFF_G + LEN_G]
        pa = _project(xb, w_a, tm, (LEN_A + PAD_A) // 7)
        pb = _project(xb, w_b, tm, LEN_B // 2)
        pc = _project(xb, w_c, tm, LEN_C // 3)
        ya = _ssd_branch(pa, b, s, ssd_conv_w[l], ssd_conv_b[l], ssd_dt_bias[l], ssd_a_log[l], ssd_d[l],
                         ssd_norm_w[l], tb)
        yb = _s5_branch(pb, b, s, s5_lambda_re[l], s5_lambda_im[l], s5_log_dt[l], s5_b_re[l], s5_b_im[l],
                        s5_c_re[l], s5_c_im[l], s5_d[l], s5_w_glu[l], s5_b_glu[l], tb)
        yc = _rwkv_branch(pc, b, s, rwkv_mu[l], rwkv_mu_lora[l], rwkv_w0[l], rwkv_w2[l], rwkv_a0[l], rwkv_a2[l],
                          rwkv_k_k[l], rwkv_k_a[l], rwkv_r_k[l], rwkv_lnx_w[l], rwkv_lnx_b[l], min(512, s))
        merged = _merge(xb, ya, yb, yc, w_g, gate_b[l], w_branch_a[l].astype(BF16), w_branch_b[l].astype(BF16),
                        w_branch_c[l].astype(BF16), tm, 256)
        xf, xb = _out_norm(merged, xf, w_out[l].astype(BF16), ln_g[l], ln_b[l], min(512, nt))
    return xf.reshape(b, s, d)
```

```python
import collections
import functools

import jax
import jax.numpy as jnp
import numpy as np
from jax import lax
from jax.experimental import pallas as pl
from jax.experimental.pallas import tpu as pltpu

F32 = jnp.float32
BF16 = jnp.bfloat16
HIGHEST = lax.Precision.HIGHEST

LANES = 128
SUBLANES = 8
MXU_WIDTH = 256
VMEM_LIMIT = 56 * 1024 * 1024

D_MODEL = 2048
DEPTH = 4
SSD_HEADS = 32
SSD_HEAD_DIM = 64
SSD_WIDTH = SSD_HEADS * SSD_HEAD_DIM
SSD_GROUPS = 8
SSD_STATE = 128
SSD_CONV = 4
SSD_CHUNK = 128
SSD_NORM_EPS = 1e-5
SSD_GROUP_W = SSD_WIDTH // SSD_GROUPS
S5_GROUP_CH = 16
S5_GROUPS = 64
S5_WIDTH = S5_GROUPS * S5_GROUP_CH
S5_STATE = 64
S5_OCTETS = S5_WIDTH // LANES
S5_OCT_STATE = (LANES // S5_GROUP_CH) * S5_STATE
RWKV_HEADS = 16
RWKV_HEAD_DIM = 64
RWKV_WIDTH = RWKV_HEADS * RWKV_HEAD_DIM
RWKV_LORA = 64
RWKV_LNX_EPS = 64e-5
RWKV_CHUNK = 64
RWKV_SUPER = 256
N_BRANCHES = 3
DEEPNORM_ALPHA = (2 * DEPTH) ** 0.25
LN_EPS = 1e-5

OFF_A = 0
LEN_A = SSD_WIDTH + (SSD_WIDTH + 2 * SSD_GROUPS * SSD_STATE) + SSD_HEADS
OFF_B = OFF_A + LEN_A
LEN_B = 2 * S5_WIDTH
OFF_C = OFF_B + LEN_B
LEN_C = 4 * RWKV_WIDTH + 2 * RWKV_LORA
OFF_G = OFF_C + LEN_C
LEN_G = N_BRANCHES * D_MODEL
PAD_A = (-LEN_A) % LANES


def _silu(x):
    return x / (1.0 + jnp.exp(-x))


def _sigmoid(x):
    return 1.0 / (1.0 + jnp.exp(-x))


def _softplus(x):
    return jnp.maximum(x, 0.0) + jnp.log(1.0 + jnp.exp(-jnp.abs(x)))


def _dot(a, b):
    return jnp.dot(a, b, preferred_element_type=F32)


def _dot_nt(a, b):
    return lax.dot_general(a, b, (((1,), (1,)), ((), ())), preferred_element_type=F32)


def _dot_hilo(a, b_bf):
    hi = a.astype(BF16)
    lo = (a - hi.astype(F32)).astype(BF16)
    return _dot(hi, b_bf) + _dot(lo, b_bf)


def _shift_rows(x, carry8, d, row8):
    sh = pltpu.roll(x, d, 0)
    head = jnp.where(row8 < d, pltpu.roll(carry8, d, 0), sh[0:SUBLANES])
    return jnp.concatenate([head, sh[SUBLANES:]], axis=0)


def _mm_kernel(x_ref, w_ref, o_ref):
    o_ref[...] = _dot(x_ref[...], w_ref[...]).astype(o_ref.dtype)


def _project(x_bf, w_bf, layer, tm, tn):
    nt, d = x_bf.shape
    n = w_bf.shape[2]
    return pl.pallas_call(
        _mm_kernel,
        grid=(nt // tm, pl.cdiv(n, tn)),
        in_specs=[pl.BlockSpec((tm, d), lambda i, j: (i, 0)),
                  pl.BlockSpec((None, d, tn), lambda i, j: (layer, 0, j))],
        out_specs=pl.BlockSpec((tm, tn), lambda i, j: (i, j)),
        out_shape=jax.ShapeDtypeStruct((nt, n), BF16),
        compiler_params=pltpu.CompilerParams(
            dimension_semantics=("parallel", "arbitrary"), vmem_limit_bytes=VMEM_LIMIT),
        name="in_proj",
    )(x_bf, w_bf)


def _ssd_kernel(z_ref, xs_ref, bc_ref, dt_ref, cwx_ref, cbx_ref, cwbc_ref, cbbc_ref,
                dtb_ref, a_ref, dsk_ref, nw_ref, exp_ref, tri_ref,
                shift_ref, y_ref, state_ref, cx_ref, cbc_ref, *, tb):
    @pl.when(pl.program_id(1) == 0)
    def _():
        state_ref[...] = jnp.zeros_like(state_ref)
        cx_ref[...] = jnp.zeros_like(cx_ref)
        cbc_ref[...] = jnp.zeros_like(cbc_ref)

    row8 = lax.broadcasted_iota(jnp.int32, (SUBLANES, 1), 0)

    def conv(u_ref, carry_ref, w_ref, b_ref):
        u_bf = u_ref[...]
        shifted = _dot(shift_ref[...], u_bf)
        u = u_bf.astype(F32)
        c8 = carry_ref[...]
        acc = u * w_ref[SSD_CONV - 1:SSD_CONV, :] + b_ref[...]
        for d in range(1, SSD_CONV):
            sh = shifted[(d - 1) * tb:d * tb]
            head = jnp.where(row8 < d, pltpu.roll(c8, d, 0), sh[0:SUBLANES])
            sh = jnp.concatenate([head, sh[SUBLANES:]], axis=0)
            acc = acc + sh * w_ref[SSD_CONV - 1 - d:SSD_CONV - d, :]
        carry_ref[...] = u[tb - SUBLANES:tb]
        return _silu(acc)

    xs = conv(xs_ref, cx_ref, cwx_ref, cbx_ref)
    bc = conv(bc_ref, cbc_ref, cwbc_ref, cbbc_ref)

    L = SSD_CHUNK
    ti = lax.broadcasted_iota(jnp.int32, (L, L), 0)
    si = lax.broadcasted_iota(jnp.int32, (L, L), 1)
    causal = ti >= si
    lane = lax.broadcasted_iota(jnp.int32, (1, LANES), 1)
    lo_half = lane < SSD_HEAD_DIM
    tri = tri_ref[...]
    expand = exp_ref[...]
    gsn = SSD_GROUPS * SSD_STATE

    for ci in range(tb // L):
        r0 = ci * L
        dt = _softplus(dt_ref[r0:r0 + L, :].astype(F32) + dtb_ref[...])
        da = dt * a_ref[...]
        cs = jnp.dot(tri, da, precision=HIGHEST, preferred_element_type=F32)
        cs_t = cs.T
        dt_t = dt.T
        e_exp = _dot_hilo(jnp.exp(cs), expand)
        for g in range(SSD_GROUPS):
            c0 = g * SSD_GROUP_W
            x_g = xs[r0:r0 + L, c0:c0 + SSD_GROUP_W]
            b_g = bc[r0:r0 + L, g * SSD_STATE:(g + 1) * SSD_STATE]
            c_g = bc[r0:r0 + L, gsn + g * SSD_STATE:gsn + (g + 1) * SSD_STATE]
            c_bf = c_g.astype(BF16)
            cb = _dot_nt(c_bf, b_g.astype(BF16))
            b_t = b_g.T
            st_old = state_ref[:, c0:c0 + SSD_GROUP_W]
            y_off = _dot(c_bf, st_old.astype(BF16))
            y_parts, new_parts = [], []
            for pr in range(2):
                xp = x_g[:, pr * LANES:(pr + 1) * LANES]
                x_half = (jnp.where(lo_half, xp, 0.0).astype(BF16), jnp.where(lo_half, 0.0, xp).astype(BF16))
                y_p = None
                n_p = None
                for k in range(2):
                    h = g * 4 + pr * 2 + k
                    col = cs[:, h:h + 1]
                    row = cs_t[h:h + 1, :]
                    dtrow = dt_t[h:h + 1, :]
                    lmat = jnp.exp(jnp.where(causal, col - row, -1e30))
                    m_h = (cb * lmat * dtrow).astype(BF16)
                    y_k = _dot(m_h, x_half[k])
                    wrow = jnp.exp(cs_t[h:h + 1, L - 1:L] - row) * dtrow
                    n_k = _dot((b_t * wrow).astype(BF16), x_half[k])
                    y_p = y_k if y_p is None else y_p + y_k
                    n_p = n_k if n_p is None else n_p + n_k
                y_parts.append(y_p)
                new_parts.append(n_p)
            y_diag = jnp.concatenate(y_parts, axis=1)
            new_g = jnp.concatenate(new_parts, axis=1)
            e_g = e_exp[:, c0:c0 + SSD_GROUP_W]
            state_ref[:, c0:c0 + SSD_GROUP_W] = st_old * e_g[L - 1:L, :] + new_g
            y = y_diag + y_off * e_g + dsk_ref[:, c0:c0 + SSD_GROUP_W] * x_g
            y = y * _silu(z_ref[r0:r0 + L, c0:c0 + SSD_GROUP_W].astype(F32))
            ms = jnp.mean(y * y, axis=-1, keepdims=True)
            y = y * lax.rsqrt(ms + SSD_NORM_EPS) * nw_ref[:, c0:c0 + SSD_GROUP_W]
            y_ref[r0:r0 + L, c0:c0 + SSD_GROUP_W] = y.astype(y_ref.dtype)


def _ssd_branch(proj_a, batch, seq, conv_w, conv_b, dt_bias, a_log, d_skip, norm_w, tb):
    nt = batch * seq
    nblk = seq // tb
    w2 = SSD_WIDTH // D_MODEL
    assert w2 == 1
    dt_blk = (2 * SSD_WIDTH + 2 * SSD_GROUPS * SSD_STATE) // LANES
    pad_h = LANES - SSD_HEADS
    dtb = jnp.pad(dt_bias, (0, pad_h)).reshape(1, LANES)
    a_neg = jnp.pad(-jnp.exp(a_log.astype(F32)), (0, pad_h)).reshape(1, LANES)
    dsk = jnp.repeat(d_skip, SSD_HEAD_DIM).reshape(1, SSD_WIDTH)
    expand = (jnp.arange(LANES)[:, None] == (jnp.arange(SSD_WIDTH)[None, :] // SSD_HEAD_DIM)).astype(BF16)
    tri = jnp.tril(jnp.ones((SSD_CHUNK, SSD_CHUNK), F32))
    t_idx = jnp.arange(tb)
    shift = jnp.concatenate([(t_idx[None, :] == t_idx[:, None] - d).astype(BF16) for d in range(1, SSD_CONV)],
                            axis=0)
    row = lambda b, i: (b * nblk + i, 0)
    full = lambda b, i: (0, 0)
    return pl.pallas_call(
        functools.partial(_ssd_kernel, tb=tb),
        grid=(batch, nblk),
        in_specs=[
            pl.BlockSpec((tb, SSD_WIDTH), lambda b, i: (b * nblk + i, 0)),
            pl.BlockSpec((tb, SSD_WIDTH), lambda b, i: (b * nblk + i, 1)),
            pl.BlockSpec((tb, SSD_WIDTH), lambda b, i: (b * nblk + i, 2)),
            pl.BlockSpec((tb, LANES), lambda b, i: (b * nblk + i, dt_blk)),
            pl.BlockSpec((SSD_CONV, SSD_WIDTH), lambda b, i: (0, 0)),
            pl.BlockSpec((1, SSD_WIDTH), lambda b, i: (0, 0)),
            pl.BlockSpec((SSD_CONV, SSD_WIDTH), lambda b, i: (0, 1)),
            pl.BlockSpec((1, SSD_WIDTH), lambda b, i: (0, 1)),
            pl.BlockSpec((1, LANES), full), pl.BlockSpec((1, LANES), full),
            pl.BlockSpec((1, SSD_WIDTH), full), pl.BlockSpec((1, SSD_WIDTH), full),
            pl.BlockSpec((LANES, SSD_WIDTH), full), pl.BlockSpec((SSD_CHUNK, SSD_CHUNK), full),
            pl.BlockSpec(((SSD_CONV - 1) * tb, tb), full),
        ],
        out_specs=pl.BlockSpec((tb, SSD_WIDTH), row),
        out_shape=jax.ShapeDtypeStruct((nt, SSD_WIDTH), BF16),
        scratch_shapes=[pltpu.VMEM((SSD_STATE, SSD_WIDTH), F32),
                        pltpu.VMEM((SUBLANES, SSD_WIDTH), F32),
                        pltpu.VMEM((SUBLANES, SSD_WIDTH), F32)],
        compiler_params=pltpu.CompilerParams(
            dimension_semantics=("parallel", "arbitrary"), vmem_limit_bytes=VMEM_LIMIT),
        name="ssd_branch",
    )(proj_a, proj_a, proj_a, proj_a, conv_w, conv_b.reshape(1, -1), conv_w, conv_b.reshape(1, -1),
      dtb, a_neg, dsk, norm_w.reshape(1, -1), expand, tri, shift)


S5_QUADS = 4
S5_BLOCKS = 2 * S5_GROUPS * S5_STATE // LANES
S5_QBLK = S5_BLOCKS // S5_QUADS


def _s5_kernel(u_ref, z_ref, bb_ref, co_ref, lr_ref, li_ref, d_ref, wglu_ref, bglu_ref,
               y_ref, scr_ref, st_ref, *, tb, tpad):
    @pl.when(pl.program_id(1) == 0)
    def _():
        st_ref[...] = jnp.zeros_like(st_ref)

    half_blk = S5_OCT_STATE // LANES
    u_bf = u_ref[...]
    for o in range(S5_OCTETS):
        bu = _dot(u_bf[:, o * LANES:(o + 1) * LANES], bb_ref[o])
        q, half = divmod(o, 2)
        for j in range(half_blk):
            jr = S5_QBLK * q + half_blk * half + j
            ji = jr + SUBLANES
            scr_ref[jr * tpad:jr * tpad + tb, :] = bu[:, j * LANES:(j + 1) * LANES]
            scr_ref[ji * tpad:ji * tpad + tb, :] = bu[:, S5_OCT_STATE + j * LANES:S5_OCT_STATE + (j + 1) * LANES]

    lr = [lr_ref[q] for q in range(S5_QUADS)]
    li = [li_ref[q] for q in range(S5_QUADS)]

    def step(t, carry):
        new = []
        for q in range(S5_QUADS):
            sr, si = carry[2 * q], carry[2 * q + 1]
            re_rows = pl.ds(S5_QBLK * q * tpad + t, SUBLANES, stride=tpad)
            im_rows = pl.ds((S5_QBLK * q + SUBLANES) * tpad + t, SUBLANES, stride=tpad)
            nr = lr[q] * sr - li[q] * si + scr_ref[re_rows, :]
            ni = lr[q] * si + li[q] * sr + scr_ref[im_rows, :]
            scr_ref[re_rows, :] = nr
            scr_ref[im_rows, :] = ni
            new += [nr, ni]
        return tuple(new)

    carry = lax.fori_loop(0, tb, step, tuple(st_ref[i] for i in range(2 * S5_QUADS)), unroll=8)
    for i in range(2 * S5_QUADS):
        st_ref[i] = carry[i]

    ys = []
    for o in range(S5_OCTETS):
        q, half = divmod(o, 2)
        blks = []
        for part in range(2):
            for j in range(half_blk):
                jb = S5_QBLK * q + SUBLANES * part + half_blk * half + j
                blks.append(scr_ref[jb * tpad:jb * tpad + tb, :].astype(BF16))
        ys.append(_dot(jnp.concatenate(blks, axis=1), co_ref[o]))
    y = jnp.concatenate(ys, axis=1) + d_ref[...] * u_bf.astype(F32)
    y = 0.5 * y * (1.0 + jnp.tanh(0.7978845608028654 * (y + 0.044715 * (y * y * y))))
    gl = _dot(y.astype(BF16), wglu_ref[...]) + bglu_ref[...]
    y = y * _sigmoid(gl)
    y_ref[...] = (y * _silu(z_ref[...].astype(F32))).astype(y_ref.dtype)


def _s5_branch(proj_b, batch, seq, lam_re, lam_im, log_dt, b_re, b_im, c_re, c_im, d_s5, w_glu, b_glu, tb):
    nt = batch * seq
    nblk = seq // tb
    tpad = tb + SUBLANES
    dt = jnp.exp(log_dt)[:, None]
    mag = jnp.exp(lam_re * dt)
    ang = lam_im * dt
    lb_re, lb_im = mag * jnp.cos(ang), mag * jnp.sin(ang)
    den = jnp.square(lam_re) + jnp.square(lam_im)
    nr, ni = lb_re - 1.0, lb_im
    q_re = (nr * lam_re + ni * lam_im) / den
    q_im = (ni * lam_re - nr * lam_im) / den
    bb_re = q_re[..., None] * b_re - q_im[..., None] * b_im
    bb_im = q_re[..., None] * b_im + q_im[..., None] * b_re
    g8 = LANES // S5_GROUP_CH
    eye = jnp.eye(g8, dtype=F32)

    def in_mat(bb):
        m = jnp.einsum('ogph,gk->oghkp', bb.reshape(S5_OCTETS, g8, S5_STATE, S5_GROUP_CH), eye)
        return m.reshape(S5_OCTETS, LANES, S5_OCT_STATE)

    def out_mat(c):
        m = jnp.einsum('oghp,gk->okpgh', c.reshape(S5_OCTETS, g8, S5_GROUP_CH, S5_STATE), eye)
        return m.reshape(S5_OCTETS, S5_OCT_STATE, LANES)

    bb = jnp.concatenate([in_mat(bb_re), in_mat(bb_im)], axis=2).astype(BF16)
    co = jnp.concatenate([out_mat(c_re), -out_mat(c_im)], axis=1).astype(BF16)
    lr = lb_re.reshape(S5_QUADS, SUBLANES, LANES)
    li = lb_im.reshape(S5_QUADS, SUBLANES, LANES)
    full2 = lambda b, i: (0, 0)
    full3 = lambda b, i: (0, 0, 0)
    return pl.pallas_call(
        functools.partial(_s5_kernel, tb=tb, tpad=tpad),
        grid=(batch, nblk),
        in_specs=[
            pl.BlockSpec((tb, S5_WIDTH), lambda b, i: (b * nblk + i, 0)),
            pl.BlockSpec((tb, S5_WIDTH), lambda b, i: (b * nblk + i, 1)),
            pl.BlockSpec((S5_OCTETS, LANES, 2 * S5_OCT_STATE), full3),
            pl.BlockSpec((S5_OCTETS, 2 * S5_OCT_STATE, LANES), full3),
            pl.BlockSpec((S5_QUADS, SUBLANES, LANES), full3),
            pl.BlockSpec((S5_QUADS, SUBLANES, LANES), full3),
            pl.BlockSpec((1, S5_WIDTH), full2),
            pl.BlockSpec((S5_WIDTH, S5_WIDTH), full2),
            pl.BlockSpec((1, S5_WIDTH), full2),
        ],
        out_specs=pl.BlockSpec((tb, S5_WIDTH), lambda b, i: (b * nblk + i, 0)),
        out_shape=jax.ShapeDtypeStruct((nt, S5_WIDTH), BF16),
        scratch_shapes=[pltpu.VMEM((S5_BLOCKS * tpad, LANES), F32),
                        pltpu.VMEM((2 * S5_QUADS, SUBLANES, LANES), F32)],
        compiler_params=pltpu.CompilerParams(
            dimension_semantics=("parallel", "arbitrary"), vmem_limit_bytes=VMEM_LIMIT),
        name="s5_branch",
    )(proj_b, proj_b, bb, co, lr, li, d_s5.reshape(1, -1), w_glu.astype(BF16), b_glu.reshape(1, -1))


def _dot_rhs_hilo(m_bf, x):
    hi = x.astype(BF16)
    lo = (x - hi.astype(F32)).astype(BF16)
    return _dot(m_bf, hi) + _dot(m_bf, lo)


def _rwkv_kernel(r_ref, k_ref, v_ref, z_ref, lo_ref, mu_ref, mulo_ref, wl_ref, w0_ref, a0_ref, kk_ref, ka_ref,
                 rk_ref, lw_ref, lb_ref, ones_ref, tri_ref, cones_ref, y_ref, st_ref, carry_ref, *, tb):
    @pl.when(pl.program_id(2) == 0)
    def _():
        st_ref[...] = jnp.zeros_like(st_ref)
        carry_ref[...] = jnp.zeros_like(carry_ref)

    L = RWKV_CHUNK
    hd = RWKV_HEAD_DIM
    row8 = lax.broadcasted_iota(jnp.int32, (SUBLANES, 1), 0)
    lane = lax.broadcasted_iota(jnp.int32, (1, LANES), 1)
    head_mask = (lane < hd, lane >= hd)

    def lerp(ref, idx, mu):
        cur = ref[...].astype(F32)
        prev = _shift_rows(cur, carry_ref[idx], 1, row8)
        carry_ref[idx] = cur[tb - SUBLANES:tb]
        return cur + (prev - cur) * mu

    r = lerp(r_ref, 0, mu_ref[0:1, :])
    k = lerp(k_ref, 1, mu_ref[1:2, :])
    v = lerp(v_ref, 2, mu_ref[2:3, :])
    zc = lerp(z_ref, 3, mu_ref[3:4, :])
    xl = lerp(lo_ref, 4, mulo_ref[...])
    lin = jnp.where(head_mask[0], jnp.tanh(xl), xl).astype(BF16)
    wa = _dot(lin, wl_ref[...])
    w_log = -_softplus(-(w0_ref[...] + wa[:, :LANES])) - 0.5
    logdec = -jnp.exp(w_log)
    a = _sigmoid(a0_ref[...] + wa[:, LANES:])
    ones = ones_ref[...]
    kkr = k * kk_ref[...]
    kk = kkr * lax.rsqrt(jnp.maximum(_dot((kkr * kkr).astype(BF16), ones), 1e-24))
    k2 = k * (1.0 + (a - 1.0) * ka_ref[...])
    av = -kk
    bv = kk * a

    sbr = RWKV_SUPER
    nsb = tb // sbr
    ti = lax.broadcasted_iota(jnp.int32, (sbr, sbr), 0)
    si = lax.broadcasted_iota(jnp.int32, (sbr, sbr), 1)
    same_chunk = (ti // L) == (si // L)
    strict = same_chunk & (si < ti)
    incl = same_chunk & (si <= ti)
    ri = lax.broadcasted_iota(jnp.int32, (LANES, LANES), 0)
    cj = lax.broadcasted_iota(jnp.int32, (LANES, LANES), 1)
    blockdiag2 = jnp.concatenate([(ri // hd) == (cj // hd)] * 2, axis=1)
    eye = ri == cj
    tri = tri_ref[...]
    cones = cones_ref[...]
    zeros_l = jnp.zeros((L, LANES), F32)

    pre = []
    for sb in range(nsb):
        sl = slice(sb * sbr, (sb + 1) * sbr)
        ld = logdec[sl]
        cum = _dot_rhs_hilo(tri, ld)
        ctot = _dot_rhs_hilo(cones, ld)
        e_neg = jnp.exp(-cum)
        e_end = jnp.exp(ctot - cum)
        pre.append(dict(rt=r[sl] * jnp.exp(cum), at=av[sl] * jnp.exp(cum - ld), kt=k2[sl] * e_neg,
                        bt=bv[sl] * e_neg, kh=k2[sl] * e_end, bh=bv[sl] * e_end, v=v[sl], ctot=ctot))

    chains = [(sb, h) for sb in range(nsb) for h in range(2)]
    xs, ps, arb, ark, vh = {}, {}, {}, {}, {}
    for c in chains:
        sb, h = c
        d = pre[sb]
        m = head_mask[h]
        at_h = jnp.where(m, d['at'], 0.0)
        lhs = jnp.concatenate([at_h, jnp.where(m, d['rt'], 0.0)], axis=0).astype(BF16)
        rhs1 = jnp.concatenate([d['bt'], d['kt']], axis=0).astype(BF16)
        a4 = _dot_nt(lhs, rhs1)
        ps[c] = jnp.where(strict, a4[0:sbr, 0:sbr], 0.0)
        aak = jnp.where(strict, a4[0:sbr, sbr:], 0.0).astype(BF16)
        arb[c] = jnp.where(incl, a4[sbr:, 0:sbr], 0.0).astype(BF16)
        ark[c] = jnp.where(incl, a4[sbr:, sbr:], 0.0).astype(BF16)
        vh[c] = jnp.where(m, d['v'], 0.0).astype(BF16)
        xs[c] = jnp.concatenate([at_h, _dot(aak, vh[c])], axis=1)
    n_dbl = L.bit_length() - 1
    for i in range(n_dbl):
        for c in chains:
            p_bf = ps[c].astype(BF16)
            xs[c] = xs[c] + _dot(p_bf, xs[c].astype(BF16))
            if i + 1 < n_dbl:
                ps[c] = _dot(p_bf, p_bf)
    w_sb, u_sb, q_sb, y0_sb = [], [], [], []
    for sb in range(nsb):
        w_s = xs[(sb, 0)][:, :LANES] + xs[(sb, 1)][:, :LANES]
        u_s = xs[(sb, 0)][:, LANES:] + xs[(sb, 1)][:, LANES:]
        q_s = pre[sb]['rt']
        y0_s = None
        for h in range(2):
            c = (sb, h)
            qy = _dot(arb[c], xs[c].astype(BF16))
            y0_h = qy[:, LANES:] + _dot(ark[c], vh[c])
            q_s = q_s + qy[:, :LANES]
            y0_s = y0_h if y0_s is None else y0_s + y0_h
        w_sb.append(w_s)
        u_sb.append(u_s)
        q_sb.append(q_s)
        y0_sb.append(y0_s)

    mns, gcols = [], []
    for ci in range(tb // L):
        sb, lo_r = divmod(ci * L, sbr)
        cs = slice(lo_r, lo_r + L)
        d = pre[sb]
        rhs2 = jnp.concatenate([jnp.concatenate([w_sb[sb][cs], u_sb[sb][cs]], axis=1),
                                jnp.concatenate([zeros_l, d['v'][cs]], axis=1)], axis=0).astype(BF16)
        lhs3 = jnp.concatenate([d['bh'][cs], d['kh'][cs]], axis=0).T.astype(BF16)
        mns.append(jnp.where(blockdiag2, _dot(lhs3, rhs2), 0.0))
        gl = jnp.exp(d['ctot'][lo_r:lo_r + 1, :])
        gcols.append(jnp.sum(jnp.where(eye, gl, 0.0), axis=1, keepdims=True))

    state = st_ref[...]
    ys = []
    for ci in range(tb // L):
        sb, lo_r = divmod(ci * L, sbr)
        cs = slice(lo_r, lo_r + L)
        ys.append(_dot(q_sb[sb][cs].astype(BF16), state.astype(BF16)) + y0_sb[sb][cs])
        state = gcols[ci] * state + _dot_rhs_hilo(mns[ci][:, :LANES].astype(BF16), state) + mns[ci][:, LANES:]
    st_ref[...] = state

    y = jnp.concatenate(ys, axis=0)
    inv_hd = 1.0 / hd
    mean = _dot_hilo(y, ones) * inv_hd
    yc = y - mean
    var = _dot((yc * yc).astype(BF16), ones) * inv_hd
    yn = yc * lax.rsqrt(var + RWKV_LNX_EPS) * lw_ref[...] + lb_ref[...]
    rk = _dot((r * k2 * rk_ref[...]).astype(BF16), ones)
    y_ref[...] = ((yn + rk * v) * _silu(zc)).astype(y_ref.dtype)


def _rwkv_branch(proj_c, batch, seq, mu, mu_lora, w0, w2, a0, a2, k_k, k_a, r_k, lnx_w, lnx_b, tb):
    nt = batch * seq
    nblk = seq // tb
    npair = RWKV_WIDTH // LANES
    cb = RWKV_WIDTH // LANES
    w2p = w2.reshape(RWKV_LORA, npair, LANES).transpose(1, 0, 2)
    a2p = a2.reshape(RWKV_LORA, npair, LANES).transpose(1, 0, 2)
    zer = jnp.zeros_like(w2p)
    wl = jnp.concatenate([jnp.concatenate([w2p, zer], axis=2), jnp.concatenate([zer, a2p], axis=2)], axis=1)
    wl = wl.reshape(npair * LANES, 2 * LANES).astype(BF16)
    ones = (jnp.arange(LANES)[:, None] // RWKV_HEAD_DIM == jnp.arange(LANES)[None, :] // RWKV_HEAD_DIM).astype(BF16)
    idx = jnp.arange(RWKV_SUPER)
    same_chunk = idx[:, None] // RWKV_CHUNK == idx[None, :] // RWKV_CHUNK
    cones = same_chunk.astype(BF16)
    tri = (same_chunk & (idx[None, :] <= idx[:, None])).astype(BF16)
    vec = lambda p: p.reshape(1, RWKV_WIDTH)
    tok = lambda off: pl.BlockSpec((tb, LANES), lambda b, h, i: (b * nblk + i, off + h))
    pvec = pl.BlockSpec((1, LANES), lambda b, h, i: (0, h))
    full = lambda b, h, i: (0, 0)
    return pl.pallas_call(
        functools.partial(_rwkv_kernel, tb=tb),
        grid=(batch, npair, nblk),
        in_specs=[
            tok(0), tok(cb), tok(2 * cb), tok(3 * cb),
            pl.BlockSpec((tb, LANES), lambda b, h, i: (b * nblk + i, 4 * cb)),
            pl.BlockSpec((4, LANES), lambda b, h, i: (0, h)),
            pl.BlockSpec((1, LANES), full),
            pl.BlockSpec((LANES, 2 * LANES), lambda b, h, i: (h, 0)),
            pvec, pvec, pvec, pvec, pvec, pvec, pvec,
            pl.BlockSpec((LANES, LANES), full),
            pl.BlockSpec((RWKV_SUPER, RWKV_SUPER), full),
            pl.BlockSpec((RWKV_SUPER, RWKV_SUPER), full),
        ],
        out_specs=pl.BlockSpec((tb, LANES), lambda b, h, i: (b * nblk + i, h)),
        out_shape=jax.ShapeDtypeStruct((nt, RWKV_WIDTH), BF16),
        scratch_shapes=[pltpu.VMEM((LANES, LANES), F32),
                        pltpu.VMEM((5, SUBLANES, LANES), F32)],
        compiler_params=pltpu.CompilerParams(
            dimension_semantics=("parallel", "parallel", "arbitrary"), vmem_limit_bytes=VMEM_LIMIT),
        name="rwkv_branch",
    )(proj_c, proj_c, proj_c, proj_c, proj_c, mu, mu_lora.reshape(1, 2 * RWKV_LORA), wl, vec(w0), vec(a0),
      vec(k_k), vec(k_a), vec(r_k), vec(lnx_w), vec(lnx_b), ones, tri, cones)


def _merge_kernel(x_ref, ya_ref, yb_ref, yc_ref, wg0_ref, wg1_ref, wg2_ref, wa_ref, wb_ref, wc_ref, gb_ref, o_ref):
    x = x_ref[...]
    m = _sigmoid(_dot(x, wg0_ref[...]) + gb_ref[0:1, :]) * _dot(ya_ref[...], wa_ref[...])
    m = m + _sigmoid(_dot(x, wg1_ref[...]) + gb_ref[1:2, :]) * _dot(yb_ref[...], wb_ref[...])
    m = m + _sigmoid(_dot(x, wg2_ref[...]) + gb_ref[2:3, :]) * _dot(yc_ref[...], wc_ref[...])
    o_ref[...] = m.astype(o_ref.dtype)


def _merge(x_bf, ya, yb, yc, wg, gate_b, wa, wb, wc, layer, tm, tn):
    nt, d = x_bf.shape
    nj = d // tn
    act = lambda w: pl.BlockSpec((tm, w), lambda i, j: (i, 0))
    wcol = lambda k, off: pl.BlockSpec((None, k, tn), lambda i, j: (layer, 0, off + j))
    return pl.pallas_call(
        _merge_kernel,
        grid=(nt // tm, nj),
        in_specs=[act(d), act(SSD_WIDTH), act(S5_WIDTH), act(RWKV_WIDTH),
                  wcol(d, 0), wcol(d, nj), wcol(d, 2 * nj),
                  wcol(SSD_WIDTH, 0), wcol(S5_WIDTH, 0), wcol(RWKV_WIDTH, 0),
                  pl.BlockSpec((N_BRANCHES, tn), lambda i, j: (0, j))],
        out_specs=pl.BlockSpec((tm, tn), lambda i, j: (i, j)),
        out_shape=jax.ShapeDtypeStruct((nt, d), BF16),
        compiler_params=pltpu.CompilerParams(
            dimension_semantics=("parallel", "arbitrary"), vmem_limit_bytes=VMEM_LIMIT),
        name="gated_merge",
    )(x_bf, ya, yb, yc, wg, wg, wg, wa, wb, wc, gate_b)


def _out_kernel(m_ref, x_ref, wo_ref, g_ref, b_ref, xo_ref, xb_ref):
    h = DEEPNORM_ALPHA * x_ref[...] + _dot(m_ref[...], wo_ref[...])
    mu = jnp.mean(h, axis=-1, keepdims=True)
    hc = h - mu
    var = jnp.mean(hc * hc, axis=-1, keepdims=True)
    y = hc * lax.rsqrt(var + LN_EPS) * g_ref[...] + b_ref[...]
    xo_ref[...] = y
    xb_ref[...] = y.astype(BF16)


def _out_norm(merged, x_f32, wo, ln_g, ln_b, layer, tm):
    nt, d = x_f32.shape
    row = pl.BlockSpec((tm, d), lambda i: (i, 0))
    vec = pl.BlockSpec((1, d), lambda i: (0, 0))
    return pl.pallas_call(
        _out_kernel,
        grid=(nt // tm,),
        in_specs=[row, row, pl.BlockSpec((None, d, d), lambda i: (layer, 0, 0)), vec, vec],
        out_specs=[row, row],
        out_shape=[jax.ShapeDtypeStruct((nt, d), F32), jax.ShapeDtypeStruct((nt, d), BF16)],
        compiler_params=pltpu.CompilerParams(
            dimension_semantics=("parallel",), vmem_limit_bytes=VMEM_LIMIT),
        name="out_norm",
    )(merged, x_f32, wo, ln_g.reshape(1, d), ln_b.reshape(1, d))


_Tiles = collections.namedtuple("_Tiles", "tm tn_a tn_b tn_c tn_merge tm_out tb_ssd tb_s5 tb_rwkv")


def _tiles(nt, seq):
    return _Tiles(tm=min(1024, nt), tn_a=5 * MXU_WIDTH, tn_b=4 * MXU_WIDTH, tn_c=6 * MXU_WIDTH,
                  tn_merge=MXU_WIDTH, tm_out=min(512, nt), tb_ssd=min(256, seq), tb_s5=min(256, seq),
                  tb_rwkv=min(4 * RWKV_SUPER, seq))


def kernel(x, w_in, ssd_conv_w, ssd_conv_b, ssd_dt_bias, ssd_a_log, ssd_d, ssd_norm_w, s5_lambda_re, s5_lambda_im, s5_log_dt, s5_b_re, s5_b_im, s5_c_re, s5_c_im, s5_d, s5_w_glu, s5_b_glu, rwkv_mu, rwkv_mu_lora, rwkv_w0, rwkv_w2, rwkv_a0, rwkv_a2, rwkv_k_k, rwkv_k_a, rwkv_r_k, rwkv_lnx_w, rwkv_lnx_b, gate_b, w_branch_a, w_branch_b, w_branch_c, w_out, ln_g, ln_b):
    b, s, d = x.shape
    nt = b * s
    t = _tiles(nt, s)
    xf = x.reshape(nt, d)
    xb = xf.astype(BF16)
    w_a = w_in[:, :, OFF_A:OFF_A + LEN_A + PAD_A].astype(BF16)
    w_b = w_in[:, :, OFF_B:OFF_B + LEN_B].astype(BF16)
    w_c = w_in[:, :, OFF_C:OFF_C + LEN_C].astype(BF16)
    w_g = w_in[:, :, OFF_G:OFF_G + LEN_G].astype(BF16)
    wb_a, wb_b, wb_c = w_branch_a.astype(BF16), w_branch_b.astype(BF16), w_branch_c.astype(BF16)
    w_o = w_out.astype(BF16)
    for l in range(DEPTH):
        pa = _project(xb, w_a, l, t.tm, t.tn_a)
        pb = _project(xb, w_b, l, t.tm, t.tn_b)
        pc = _project(xb, w_c, l, t.tm, t.tn_c)
        ya = _ssd_branch(pa, b, s, ssd_conv_w[l], ssd_conv_b[l], ssd_dt_bias[l], ssd_a_log[l], ssd_d[l],
                         ssd_norm_w[l], t.tb_ssd)
        yb = _s5_branch(pb, b, s, s5_lambda_re[l], s5_lambda_im[l], s5_log_dt[l], s5_b_re[l], s5_b_im[l],
                        s5_c_re[l], s5_c_im[l], s5_d[l], s5_w_glu[l], s5_b_glu[l], t.tb_s5)
        yc = _rwkv_branch(pc, b, s, rwkv_mu[l], rwkv_mu_lora[l], rwkv_w0[l], rwkv_w2[l], rwkv_a0[l], rwkv_a2[l],
                          rwkv_k_k[l], rwkv_k_a[l], rwkv_r_k[l], rwkv_lnx_w[l], rwkv_lnx_b[l], t.tb_rwkv)
        merged = _merge(xb, ya, yb, yc, w_g, gate_b[l], wb_a, wb_b, wb_c, l, t.tm, t.tn_merge)
        xf, xb = _out_norm(merged, xf, w_o, ln_g[l], ln_b[l], l, t.tm_out)
    return xf.reshape(b, s, d)
```

```python
import collections
import functools

import jax
import jax.numpy as jnp
import numpy as np
from jax import lax
from jax.experimental import pallas as pl
from jax.experimental.pallas import tpu as pltpu

F32 = jnp.float32
BF16 = jnp.bfloat16
HIGHEST = lax.Precision.HIGHEST

LANES = 128
SUBLANES = 8
MXU_WIDTH = 256
VMEM_LIMIT = 56 * 1024 * 1024

D_MODEL = 2048
DEPTH = 4
SSD_HEADS = 32
SSD_HEAD_DIM = 64
SSD_WIDTH = SSD_HEADS * SSD_HEAD_DIM
SSD_GROUPS = 8
SSD_STATE = 128
SSD_CONV = 4
SSD_CHUNK = 128
SSD_NORM_EPS = 1e-5
SSD_GROUP_W = SSD_WIDTH // SSD_GROUPS
S5_GROUP_CH = 16
S5_GROUPS = 64
S5_WIDTH = S5_GROUPS * S5_GROUP_CH
S5_STATE = 64
S5_OCTETS = S5_WIDTH // LANES
S5_OCT_STATE = (LANES // S5_GROUP_CH) * S5_STATE
RWKV_HEADS = 16
RWKV_HEAD_DIM = 64
RWKV_WIDTH = RWKV_HEADS * RWKV_HEAD_DIM
RWKV_LORA = 64
RWKV_LNX_EPS = 64e-5
RWKV_CHUNK = 64
RWKV_SUPER = 256
N_BRANCHES = 3
DEEPNORM_ALPHA = (2 * DEPTH) ** 0.25
LN_EPS = 1e-5

OFF_A = 0
LEN_A = SSD_WIDTH + (SSD_WIDTH + 2 * SSD_GROUPS * SSD_STATE) + SSD_HEADS
OFF_B = OFF_A + LEN_A
LEN_B = 2 * S5_WIDTH
OFF_C = OFF_B + LEN_B
LEN_C = 4 * RWKV_WIDTH + 2 * RWKV_LORA
OFF_G = OFF_C + LEN_C
LEN_G = N_BRANCHES * D_MODEL
PAD_A = (-LEN_A) % LANES


def _silu(x):
    return x / (1.0 + jnp.exp(-x))


def _sigmoid(x):
    return 1.0 / (1.0 + jnp.exp(-x))


def _softplus(x):
    return jnp.maximum(x, 0.0) + jnp.log(1.0 + jnp.exp(-jnp.abs(x)))


def _dot(a, b):
    return jnp.dot(a, b, preferred_element_type=F32)


def _dot_nt(a, b):
    return lax.dot_general(a, b, (((1,), (1,)), ((), ())), preferred_element_type=F32)


def _dot_hilo(a, b_bf):
    hi = a.astype(BF16)
    lo = (a - hi.astype(F32)).astype(BF16)
    return _dot(hi, b_bf) + _dot(lo, b_bf)


def _shift_rows(x, carry8, d, row8):
    sh = pltpu.roll(x, d, 0)
    head = jnp.where(row8 < d, pltpu.roll(carry8, d, 0), sh[0:SUBLANES])
    return jnp.concatenate([head, sh[SUBLANES:]], axis=0)


def _mm_kernel(x_ref, w_ref, o_ref):
    o_ref[...] = _dot(x_ref[...], w_ref[...]).astype(o_ref.dtype)


def _project(x_bf, w_bf, layer, tm, tn):
    nt, d = x_bf.shape
    n = w_bf.shape[2]
    return pl.pallas_call(
        _mm_kernel,
        grid=(nt // tm, pl.cdiv(n, tn)),
        in_specs=[pl.BlockSpec((tm, d), lambda i, j: (i, 0)),
                  pl.BlockSpec((None, d, tn), lambda i, j: (layer, 0, j))],
        out_specs=pl.BlockSpec((tm, tn), lambda i, j: (i, j)),
        out_shape=jax.ShapeDtypeStruct((nt, n), BF16),
        compiler_params=pltpu.CompilerParams(
            dimension_semantics=("parallel", "arbitrary"), vmem_limit_bytes=VMEM_LIMIT),
        name="in_proj",
    )(x_bf, w_bf)


def _slab_kernel(w_ref, a_ref, b_ref, c_ref, g_ref):
    a_ref[...] = w_ref[:, OFF_A:OFF_A + LEN_A + PAD_A].astype(BF16)
    b_ref[...] = w_ref[:, OFF_B:OFF_B + LEN_B].astype(BF16)
    c_ref[...] = w_ref[:, OFF_C:OFF_C + LEN_C].astype(BF16)
    g_ref[...] = w_ref[:, OFF_G:OFF_G + LEN_G].astype(BF16)


def _weight_slabs(w_in, tr):
    nl, d, n = w_in.shape
    lens = (LEN_A + PAD_A, LEN_B, LEN_C, LEN_G)
    return pl.pallas_call(
        _slab_kernel,
        grid=(nl, d // tr),
        in_specs=[pl.BlockSpec((None, tr, n), lambda l, i: (l, i, 0))],
        out_specs=[pl.BlockSpec((None, tr, w), lambda l, i: (l, i, 0)) for w in lens],
        out_shape=[jax.ShapeDtypeStruct((nl, d, w), BF16) for w in lens],
        compiler_params=pltpu.CompilerParams(
            dimension_semantics=("parallel", "parallel"), vmem_limit_bytes=VMEM_LIMIT),
        name="weight_slabs",
    )(w_in)


def _ssd_kernel(z_ref, xs_ref, bc_ref, dt_ref, cwx_ref, cbx_ref, cwbc_ref, cbbc_ref,
                dtb_ref, a_ref, dsk_ref, nw_ref, exp_ref, tri_ref,
                shift_ref, y_ref, state_ref, cx_ref, cbc_ref, *, tb):
    @pl.when(pl.program_id(1) == 0)
    def _():
        state_ref[...] = jnp.zeros_like(state_ref)
        cx_ref[...] = jnp.zeros_like(cx_ref)
        cbc_ref[...] = jnp.zeros_like(cbc_ref)

    row8 = lax.broadcasted_iota(jnp.int32, (SUBLANES, 1), 0)

    def conv(u_ref, carry_ref, w_ref, b_ref):
        u_bf = u_ref[...]
        shifted = _dot(shift_ref[...], u_bf)
        u = u_bf.astype(F32)
        c8 = carry_ref[...]
        acc = u * w_ref[SSD_CONV - 1:SSD_CONV, :] + b_ref[...]
        for d in range(1, SSD_CONV):
            sh = shifted[(d - 1) * tb:d * tb]
            head = jnp.where(row8 < d, pltpu.roll(c8, d, 0), sh[0:SUBLANES])
            sh = jnp.concatenate([head, sh[SUBLANES:]], axis=0)
            acc = acc + sh * w_ref[SSD_CONV - 1 - d:SSD_CONV - d, :]
        carry_ref[...] = u[tb - SUBLANES:tb]
        return _silu(acc)

    xs = conv(xs_ref, cx_ref, cwx_ref, cbx_ref)
    bc = conv(bc_ref, cbc_ref, cwbc_ref, cbbc_ref)

    L = SSD_CHUNK
    ti = lax.broadcasted_iota(jnp.int32, (L, L), 0)
    si = lax.broadcasted_iota(jnp.int32, (L, L), 1)
    causal = ti >= si
    lane = lax.broadcasted_iota(jnp.int32, (1, LANES), 1)
    lo_half = lane < SSD_HEAD_DIM
    tri = tri_ref[...]
    expand = exp_ref[...]
    gsn = SSD_GROUPS * SSD_STATE

    for ci in range(tb // L):
        r0 = ci * L
        dt = _softplus(dt_ref[r0:r0 + L, :].astype(F32) + dtb_ref[...])
        da = dt * a_ref[...]
        cs = jnp.dot(tri, da, precision=HIGHEST, preferred_element_type=F32)
        cs_t = cs.T
        dt_t = dt.T
        e_exp = _dot_hilo(jnp.exp(cs), expand)
        for g in range(SSD_GROUPS):
            c0 = g * SSD_GROUP_W
            x_g = xs[r0:r0 + L, c0:c0 + SSD_GROUP_W]
            b_g = bc[r0:r0 + L, g * SSD_STATE:(g + 1) * SSD_STATE]
            c_g = bc[r0:r0 + L, gsn + g * SSD_STATE:gsn + (g + 1) * SSD_STATE]
            c_bf = c_g.astype(BF16)
            cb = _dot_nt(c_bf, b_g.astype(BF16))
            b_t = b_g.T
            st_old = state_ref[:, c0:c0 + SSD_GROUP_W]
            y_off = _dot(c_bf, st_old.astype(BF16))
            y_parts, new_parts = [], []
            for pr in range(2):
                xp = x_g[:, pr * LANES:(pr + 1) * LANES]
                x_half = (jnp.where(lo_half, xp, 0.0).astype(BF16), jnp.where(lo_half, 0.0, xp).astype(BF16))
                y_p = None
                n_p = None
                for k in range(2):
                    h = g * 4 + pr * 2 + k
                    col = cs[:, h:h + 1]
                    row = cs_t[h:h + 1, :]
                    dtrow = dt_t[h:h + 1, :]
                    lmat = jnp.exp(jnp.where(causal, col - row, -1e30))
                    m_h = (cb * lmat * dtrow).astype(BF16)
                    y_k = _dot(m_h, x_half[k])
                    wrow = jnp.exp(cs_t[h:h + 1, L - 1:L] - row) * dtrow
                    n_k = _dot((b_t * wrow).astype(BF16), x_half[k])
                    y_p = y_k if y_p is None else y_p + y_k
                    n_p = n_k if n_p is None else n_p + n_k
                y_parts.append(y_p)
                new_parts.append(n_p)
            y_diag = jnp.concatenate(y_parts, axis=1)
            new_g = jnp.concatenate(new_parts, axis=1)
            e_g = e_exp[:, c0:c0 + SSD_GROUP_W]
            state_ref[:, c0:c0 + SSD_GROUP_W] = st_old * e_g[L - 1:L, :] + new_g
            y = y_diag + y_off * e_g + dsk_ref[:, c0:c0 + SSD_GROUP_W] * x_g
            y = y * _silu(z_ref[r0:r0 + L, c0:c0 + SSD_GROUP_W].astype(F32))
            ms = jnp.mean(y * y, axis=-1, keepdims=True)
            y = y * lax.rsqrt(ms + SSD_NORM_EPS) * nw_ref[:, c0:c0 + SSD_GROUP_W]
            y_ref[r0:r0 + L, c0:c0 + SSD_GROUP_W] = y.astype(y_ref.dtype)


def _ssd_branch(proj_a, batch, seq, conv_w, conv_b, dt_bias, a_log, d_skip, norm_w, tb):
    nt = batch * seq
    nblk = seq // tb
    w2 = SSD_WIDTH // D_MODEL
    assert w2 == 1
    dt_blk = (2 * SSD_WIDTH + 2 * SSD_GROUPS * SSD_STATE) // LANES
    pad_h = LANES - SSD_HEADS
    dtb = jnp.pad(dt_bias, (0, pad_h)).reshape(1, LANES)
    a_neg = jnp.pad(-jnp.exp(a_log.astype(F32)), (0, pad_h)).reshape(1, LANES)
    dsk = jnp.repeat(d_skip, SSD_HEAD_DIM).reshape(1, SSD_WIDTH)
    expand = (jnp.arange(LANES)[:, None] == (jnp.arange(SSD_WIDTH)[None, :] // SSD_HEAD_DIM)).astype(BF16)
    tri = jnp.tril(jnp.ones((SSD_CHUNK, SSD_CHUNK), F32))
    t_idx = jnp.arange(tb)
    shift = jnp.concatenate([(t_idx[None, :] == t_idx[:, None] - d).astype(BF16) for d in range(1, SSD_CONV)],
                            axis=0)
    row = lambda b, i: (b * nblk + i, 0)
    full = lambda b, i: (0, 0)
    return pl.pallas_call(
        functools.partial(_ssd_kernel, tb=tb),
        grid=(batch, nblk),
        in_specs=[
            pl.BlockSpec((tb, SSD_WIDTH), lambda b, i: (b * nblk + i, 0)),
            pl.BlockSpec((tb, SSD_WIDTH), lambda b, i: (b * nblk + i, 1)),
            pl.BlockSpec((tb, SSD_WIDTH), lambda b, i: (b * nblk + i, 2)),
            pl.BlockSpec((tb, LANES), lambda b, i: (b * nblk + i, dt_blk)),
            pl.BlockSpec((SSD_CONV, SSD_WIDTH), lambda b, i: (0, 0)),
            pl.BlockSpec((1, SSD_WIDTH), lambda b, i: (0, 0)),
            pl.BlockSpec((SSD_CONV, SSD_WIDTH), lambda b, i: (0, 1)),
            pl.BlockSpec((1, SSD_WIDTH), lambda b, i: (0, 1)),
            pl.BlockSpec((1, LANES), full), pl.BlockSpec((1, LANES), full),
            pl.BlockSpec((1, SSD_WIDTH), full), pl.BlockSpec((1, SSD_WIDTH), full),
            pl.BlockSpec((LANES, SSD_WIDTH), full), pl.BlockSpec((SSD_CHUNK, SSD_CHUNK), full),
            pl.BlockSpec(((SSD_CONV - 1) * tb, tb), full),
        ],
        out_specs=pl.BlockSpec((tb, SSD_WIDTH), row),
        out_shape=jax.ShapeDtypeStruct((nt, SSD_WIDTH), BF16),
        scratch_shapes=[pltpu.VMEM((SSD_STATE, SSD_WIDTH), F32),
                        pltpu.VMEM((SUBLANES, SSD_WIDTH), F32),
                        pltpu.VMEM((SUBLANES, SSD_WIDTH), F32)],
        compiler_params=pltpu.CompilerParams(
            dimension_semantics=("parallel", "arbitrary"), vmem_limit_bytes=VMEM_LIMIT),
        name="ssd_branch",
    )(proj_a, proj_a, proj_a, proj_a, conv_w, conv_b.reshape(1, -1), conv_w, conv_b.reshape(1, -1),
      dtb, a_neg, dsk, norm_w.reshape(1, -1), expand, tri, shift)


S5_QUADS = 4
S5_BLOCKS = 2 * S5_GROUPS * S5_STATE // LANES
S5_QBLK = S5_BLOCKS // S5_QUADS


def _s5_kernel(u_ref, z_ref, bb_ref, co_ref, lr_ref, li_ref, d_ref, wglu_ref, bglu_ref,
               y_ref, scr_ref, st_ref, *, tb, tpad, nbatch):
    @pl.when(pl.program_id(0) == 0)
    def _():
        st_ref[...] = jnp.zeros_like(st_ref)

    half_blk = S5_OCT_STATE // LANES
    nvec = 2 * S5_QUADS
    for b in range(nbatch):
        u_bf = u_ref[b]
        for o in range(S5_OCTETS):
            bu = _dot(u_bf[:, o * LANES:(o + 1) * LANES], bb_ref[o])
            q, half = divmod(o, 2)
            for j in range(half_blk):
                jr = S5_QBLK * q + half_blk * half + j
                ji = jr + SUBLANES
                scr_ref[b, jr * tpad:jr * tpad + tb, :] = bu[:, j * LANES:(j + 1) * LANES]
                scr_ref[b, ji * tpad:ji * tpad + tb, :] = bu[:, S5_OCT_STATE + j * LANES:
                                                              S5_OCT_STATE + (j + 1) * LANES]

    lr = [lr_ref[q] for q in range(S5_QUADS)]
    li = [li_ref[q] for q in range(S5_QUADS)]

    def step(t, carry):
        new = []
        for b in range(nbatch):
            for q in range(S5_QUADS):
                sr, si = carry[nvec * b + 2 * q], carry[nvec * b + 2 * q + 1]
                re_rows = pl.ds(S5_QBLK * q * tpad + t, SUBLANES, stride=tpad)
                im_rows = pl.ds((S5_QBLK * q + SUBLANES) * tpad + t, SUBLANES, stride=tpad)
                nr = lr[q] * sr - li[q] * si + scr_ref[b, re_rows, :]
                ni = lr[q] * si + li[q] * sr + scr_ref[b, im_rows, :]
                scr_ref[b, re_rows, :] = nr
                scr_ref[b, im_rows, :] = ni
                new += [nr, ni]
        return tuple(new)

    carry = lax.fori_loop(0, tb, step, tuple(st_ref[i] for i in range(nbatch * nvec)), unroll=8)
    for i in range(nbatch * nvec):
        st_ref[i] = carry[i]

    for b in range(nbatch):
        ys = []
        for o in range(S5_OCTETS):
            q, half = divmod(o, 2)
            blks = []
            for part in range(2):
                for j in range(half_blk):
                    jb = S5_QBLK * q + SUBLANES * part + half_blk * half + j
                    blks.append(scr_ref[b, jb * tpad:jb * tpad + tb, :].astype(BF16))
            ys.append(_dot(jnp.concatenate(blks, axis=1), co_ref[o]))
        y = jnp.concatenate(ys, axis=1) + d_ref[...] * u_ref[b].astype(F32)
        y = 0.5 * y * (1.0 + jnp.tanh(0.7978845608028654 * (y + 0.044715 * (y * y * y))))
        gl = _dot(y.astype(BF16), wglu_ref[...]) + bglu_ref[...]
        y = y * _sigmoid(gl)
        y_ref[b] = (y * _silu(z_ref[b].astype(F32))).astype(y_ref.dtype)


def _s5_branch(proj_b, batch, seq, lam_re, lam_im, log_dt, b_re, b_im, c_re, c_im, d_s5, w_glu, b_glu, tb):
    nt = batch * seq
    nblk = seq // tb
    tpad = tb + SUBLANES
    dt = jnp.exp(log_dt)[:, None]
    mag = jnp.exp(lam_re * dt)
    ang = lam_im * dt
    lb_re, lb_im = mag * jnp.cos(ang), mag * jnp.sin(ang)
    den = jnp.square(lam_re) + jnp.square(lam_im)
    nr, ni = lb_re - 1.0, lb_im
    q_re = (nr * lam_re + ni * lam_im) / den
    q_im = (ni * lam_re - nr * lam_im) / den
    bb_re = q_re[..., None] * b_re - q_im[..., None] * b_im
    bb_im = q_re[..., None] * b_im + q_im[..., None] * b_re
    g8 = LANES // S5_GROUP_CH
    eye = jnp.eye(g8, dtype=F32)

    def in_mat(bb):
        m = jnp.einsum('ogph,gk->oghkp', bb.reshape(S5_OCTETS, g8, S5_STATE, S5_GROUP_CH), eye)
        return m.reshape(S5_OCTETS, LANES, S5_OCT_STATE)

    def out_mat(c):
        m = jnp.einsum('oghp,gk->okpgh', c.reshape(S5_OCTETS, g8, S5_GROUP_CH, S5_STATE), eye)
        return m.reshape(S5_OCTETS, S5_OCT_STATE, LANES)

    bb = jnp.concatenate([in_mat(bb_re), in_mat(bb_im)], axis=2).astype(BF16)
    co = jnp.concatenate([out_mat(c_re), -out_mat(c_im)], axis=1).astype(BF16)
    lr = lb_re.reshape(S5_QUADS, SUBLANES, LANES)
    li = lb_im.reshape(S5_QUADS, SUBLANES, LANES)
    full2 = lambda i: (0, 0)
    full3 = lambda i: (0, 0, 0)
    proj3 = proj_b.reshape(batch, seq, proj_b.shape[1])
    out = pl.pallas_call(
        functools.partial(_s5_kernel, tb=tb, tpad=tpad, nbatch=batch),
        grid=(nblk,),
        in_specs=[
            pl.BlockSpec((batch, tb, S5_WIDTH), lambda i: (0, i, 0)),
            pl.BlockSpec((batch, tb, S5_WIDTH), lambda i: (0, i, 1)),
            pl.BlockSpec((S5_OCTETS, LANES, 2 * S5_OCT_STATE), full3),
            pl.BlockSpec((S5_OCTETS, 2 * S5_OCT_STATE, LANES), full3),
            pl.BlockSpec((S5_QUADS, SUBLANES, LANES), full3),
            pl.BlockSpec((S5_QUADS, SUBLANES, LANES), full3),
            pl.BlockSpec((1, S5_WIDTH), full2),
            pl.BlockSpec((S5_WIDTH, S5_WIDTH), full2),
            pl.BlockSpec((1, S5_WIDTH), full2),
        ],
        out_specs=pl.BlockSpec((batch, tb, S5_WIDTH), lambda i: (0, i, 0)),
        out_shape=jax.ShapeDtypeStruct((batch, seq, S5_WIDTH), BF16),
        scratch_shapes=[pltpu.VMEM((batch, S5_BLOCKS * tpad, LANES), F32),
                        pltpu.VMEM((batch * 2 * S5_QUADS, SUBLANES, LANES), F32)],
        compiler_params=pltpu.CompilerParams(
            dimension_semantics=("arbitrary",), vmem_limit_bytes=VMEM_LIMIT),
        name="s5_branch",
    )(proj3, proj3, bb, co, lr, li, d_s5.reshape(1, -1), w_glu.astype(BF16), b_glu.reshape(1, -1))
    return out.reshape(nt, S5_WIDTH)


def _dot_rhs_hilo(m_bf, x):
    hi = x.astype(BF16)
    lo = (x - hi.astype(F32)).astype(BF16)
    return _dot(m_bf, hi) + _dot(m_bf, lo)


def _rwkv_kernel(r_ref, k_ref, v_ref, z_ref, lo_ref, mu_ref, mulo_ref, wl_ref, w0_ref, a0_ref, kk_ref, ka_ref,
                 rk_ref, lw_ref, lb_ref, ones_ref, tri_ref, cones_ref, y_ref, st_ref, carry_ref, *, tb, nbatch,
                 npp):
    @pl.when(pl.program_id(1) == 0)
    def _():
        st_ref[...] = jnp.zeros_like(st_ref)
        carry_ref[...] = jnp.zeros_like(carry_ref)

    L = RWKV_CHUNK
    hd = RWKV_HEAD_DIM
    row8 = lax.broadcasted_iota(jnp.int32, (SUBLANES, 1), 0)
    lane = lax.broadcasted_iota(jnp.int32, (1, LANES), 1)
    head_mask = (lane < hd, lane >= hd)
    ones = ones_ref[...]

    def lerp(cur, cslot, mu):
        prev = _shift_rows(cur, carry_ref[cslot], 1, row8)
        carry_ref[cslot] = cur[tb - SUBLANES:tb]
        return cur + (prev - cur) * mu

    seqs = [(b, p) for b in range(nbatch) for p in range(npp)]
    sq = []
    for b in range(nbatch):
        xl = lerp(lo_ref[b].astype(F32), (b, 4, 0), mulo_ref[...])
        lin = jnp.where(head_mask[0], jnp.tanh(xl), xl).astype(BF16)
        for p in range(npp):
            ls = slice(p * LANES, (p + 1) * LANES)
            r = lerp(r_ref[b, :, ls].astype(F32), (b, 0, p), mu_ref[0:1, ls])
            k = lerp(k_ref[b, :, ls].astype(F32), (b, 1, p), mu_ref[1:2, ls])
            v = lerp(v_ref[b, :, ls].astype(F32), (b, 2, p), mu_ref[2:3, ls])
            zc = lerp(z_ref[b, :, ls].astype(F32), (b, 3, p), mu_ref[3:4, ls])
            wa = _dot(lin, wl_ref[ls, :])
            w_log = -_softplus(-(w0_ref[:, ls] + wa[:, :LANES])) - 0.5
            a = _sigmoid(a0_ref[:, ls] + wa[:, LANES:])
            kkr = k * kk_ref[:, ls]
            kk = kkr * lax.rsqrt(jnp.maximum(_dot((kkr * kkr).astype(BF16), ones), 1e-24))
            sq.append(dict(r=r, v=v, zc=zc, k2=k * (1.0 + (a - 1.0) * ka_ref[:, ls]), av=-kk, bv=kk * a,
                           logdec=-jnp.exp(w_log)))

    sbr = RWKV_SUPER
    nsb = tb // sbr
    ti = lax.broadcasted_iota(jnp.int32, (sbr, sbr), 0)
    si = lax.broadcasted_iota(jnp.int32, (sbr, sbr), 1)
    same_chunk = (ti // L) == (si // L)
    strict = same_chunk & (si < ti)
    incl = same_chunk & (si <= ti)
    ri = lax.broadcasted_iota(jnp.int32, (LANES, LANES), 0)
    cj = lax.broadcasted_iota(jnp.int32, (LANES, LANES), 1)
    blockdiag2 = jnp.concatenate([(ri // hd) == (cj // hd)] * 2, axis=1)
    eye = ri == cj
    tri = tri_ref[...]
    cones = cones_ref[...]
    zeros_l = jnp.zeros((L, LANES), F32)

    units = [(s, sb) for s in range(len(seqs)) for sb in range(nsb)]
    pre = {}
    for u in units:
        s, sb = u
        q = sq[s]
        sl = slice(sb * sbr, (sb + 1) * sbr)
        ld = q['logdec'][sl]
        cum = _dot_rhs_hilo(tri, ld)
        ctot = _dot_rhs_hilo(cones, ld)
        e_neg = jnp.exp(-cum)
        e_end = jnp.exp(ctot - cum)
        pre[u] = dict(rt=q['r'][sl] * jnp.exp(cum), at=q['av'][sl] * jnp.exp(cum - ld), kt=q['k2'][sl] * e_neg,
                      bt=q['bv'][sl] * e_neg, kh=q['k2'][sl] * e_end, bh=q['bv'][sl] * e_end, v=q['v'][sl],
                      ctot=ctot)

    chains = [(u, h) for u in units for h in range(2)]
    xs, ps, arb, ark, vh = {}, {}, {}, {}, {}
    for c in chains:
        sb, h = c
        d = pre[sb]
        m = head_mask[h]
        at_h = jnp.where(m, d['at'], 0.0)
        lhs = jnp.concatenate([at_h, jnp.where(m, d['rt'], 0.0)], axis=0).astype(BF16)
        rhs1 = jnp.concatenate([d['bt'], d['kt']], axis=0).astype(BF16)
        a4 = _dot_nt(lhs, rhs1)
        ps[c] = jnp.where(strict, a4[0:sbr, 0:sbr], 0.0)
        aak = jnp.where(strict, a4[0:sbr, sbr:], 0.0).astype(BF16)
        arb[c] = jnp.where(incl, a4[sbr:, 0:sbr], 0.0).astype(BF16)
        ark[c] = jnp.where(incl, a4[sbr:, sbr:], 0.0).astype(BF16)
        vh[c] = jnp.where(m, d['v'], 0.0).astype(BF16)
        xs[c] = jnp.concatenate([at_h, _dot(aak, vh[c])], axis=1)
    n_dbl = L.bit_length() - 1
    for i in range(n_dbl):
        for c in chains:
            p_bf = ps[c].astype(BF16)
            xs[c] = xs[c] + _dot(p_bf, xs[c].astype(BF16))
            if i + 1 < n_dbl:
                ps[c] = _dot(p_bf, p_bf)
    w_sb, u_sb, q_sb, y0_sb = {}, {}, {}, {}
    for sb in units:
        w_s = xs[(sb, 0)][:, :LANES] + xs[(sb, 1)][:, :LANES]
        u_s = xs[(sb, 0)][:, LANES:] + xs[(sb, 1)][:, LANES:]
        q_s = pre[sb]['rt']
        y0_s = None
        for h in range(2):
            c = (sb, h)
            qy = _dot(arb[c], xs[c].astype(BF16))
            y0_h = qy[:, LANES:] + _dot(ark[c], vh[c])
            q_s = q_s + qy[:, :LANES]
            y0_s = y0_h if y0_s is None else y0_s + y0_h
        w_sb[sb] = w_s
        u_sb[sb] = u_s
        q_sb[sb] = q_s
        y0_sb[sb] = y0_s

    nchunk = tb // L
    mns, gcols = {}, {}
    for ci in range(nchunk):
        sbi, lo_r = divmod(ci * L, sbr)
        cs = slice(lo_r, lo_r + L)
        for s in range(len(seqs)):
            u = (s, sbi)
            d = pre[u]
            rhs2 = jnp.concatenate([jnp.concatenate([w_sb[u][cs], u_sb[u][cs]], axis=1),
                                    jnp.concatenate([zeros_l, d['v'][cs]], axis=1)], axis=0).astype(BF16)
            lhs3 = jnp.concatenate([d['bh'][cs], d['kh'][cs]], axis=0).T.astype(BF16)
            mns[(s, ci)] = jnp.where(blockdiag2, _dot(lhs3, rhs2), 0.0)
            gl = jnp.exp(d['ctot'][lo_r:lo_r + 1, :])
            gcols[(s, ci)] = jnp.sum(jnp.where(eye, gl, 0.0), axis=1, keepdims=True)

    states = [st_ref[s] for s in range(len(seqs))]
    ys = [[] for _ in seqs]
    for ci in range(nchunk):
        sbi, lo_r = divmod(ci * L, sbr)
        cs = slice(lo_r, lo_r + L)
        for s in range(len(seqs)):
            u = (s, sbi)
            st = states[s]
            ys[s].append(_dot(q_sb[u][cs].astype(BF16), st.astype(BF16)) + y0_sb[u][cs])
            mn = mns[(s, ci)]
            states[s] = gcols[(s, ci)] * st + _dot_rhs_hilo(mn[:, :LANES].astype(BF16), st) + mn[:, LANES:]

    inv_hd = 1.0 / hd
    for s, (b, p) in enumerate(seqs):
        st_ref[s] = states[s]
        q = sq[s]
        ls = slice(p * LANES, (p + 1) * LANES)
        y = jnp.concatenate(ys[s], axis=0)
        mean = _dot_hilo(y, ones) * inv_hd
        yc = y - mean
        var = _dot((yc * yc).astype(BF16), ones) * inv_hd
        yn = yc * lax.rsqrt(var + RWKV_LNX_EPS) * lw_ref[:, ls] + lb_ref[:, ls]
        rk = _dot((q['r'] * q['k2'] * rk_ref[:, ls]).astype(BF16), ones)
        y_ref[b, :, ls] = ((yn + rk * q['v']) * _silu(q['zc'])).astype(y_ref.dtype)


def _rwkv_branch(proj_c, batch, seq, mu, mu_lora, w0, w2, a0, a2, k_k, k_a, r_k, lnx_w, lnx_b, tb, npp):
    nt = batch * seq
    nblk = seq // tb
    npair = RWKV_WIDTH // LANES
    wblk = npp * LANES
    cb = RWKV_WIDTH // wblk
    proj3 = proj_c.reshape(batch, seq, proj_c.shape[1])
    w2p = w2.reshape(RWKV_LORA, npair, LANES).transpose(1, 0, 2)
    a2p = a2.reshape(RWKV_LORA, npair, LANES).transpose(1, 0, 2)
    zer = jnp.zeros_like(w2p)
    wl = jnp.concatenate([jnp.concatenate([w2p, zer], axis=2), jnp.concatenate([zer, a2p], axis=2)], axis=1)
    wl = wl.reshape(npair * LANES, 2 * LANES).astype(BF16)
    ones = (jnp.arange(LANES)[:, None] // RWKV_HEAD_DIM == jnp.arange(LANES)[None, :] // RWKV_HEAD_DIM).astype(BF16)
    idx = jnp.arange(RWKV_SUPER)
    same_chunk = idx[:, None] // RWKV_CHUNK == idx[None, :] // RWKV_CHUNK
    cones = same_chunk.astype(BF16)
    tri = (same_chunk & (idx[None, :] <= idx[:, None])).astype(BF16)
    vec = lambda p: p.reshape(1, RWKV_WIDTH)
    tok = lambda off: pl.BlockSpec((batch, tb, wblk), lambda h, i: (0, i, off + h))
    pvec = pl.BlockSpec((1, wblk), lambda h, i: (0, h))
    full = lambda h, i: (0, 0)
    out = pl.pallas_call(
        functools.partial(_rwkv_kernel, tb=tb, nbatch=batch, npp=npp),
        grid=(npair // npp, nblk),
        in_specs=[
            tok(0), tok(cb), tok(2 * cb), tok(3 * cb),
            pl.BlockSpec((batch, tb, LANES), lambda h, i: (0, i, 4 * RWKV_WIDTH // LANES)),
            pl.BlockSpec((4, wblk), lambda h, i: (0, h)),
            pl.BlockSpec((1, LANES), full),
            pl.BlockSpec((wblk, 2 * LANES), lambda h, i: (h, 0)),
            pvec, pvec, pvec, pvec, pvec, pvec, pvec,
            pl.BlockSpec((LANES, LANES), full),
            pl.BlockSpec((RWKV_SUPER, RWKV_SUPER), full),
            pl.BlockSpec((RWKV_SUPER, RWKV_SUPER), full),
        ],
        out_specs=pl.BlockSpec((batch, tb, wblk), lambda h, i: (0, i, h)),
        out_shape=jax.ShapeDtypeStruct((batch, seq, RWKV_WIDTH), BF16),
        scratch_shapes=[pltpu.VMEM((batch * npp, LANES, LANES), F32),
                        pltpu.VMEM((batch, 5, npp, SUBLANES, LANES), F32)],
        compiler_params=pltpu.CompilerParams(
            dimension_semantics=("parallel", "arbitrary"), vmem_limit_bytes=VMEM_LIMIT),
        name="rwkv_branch",
    )(proj3, proj3, proj3, proj3, proj3, mu, mu_lora.reshape(1, 2 * RWKV_LORA), wl, vec(w0), vec(a0),
      vec(k_k), vec(k_a), vec(r_k), vec(lnx_w), vec(lnx_b), ones, tri, cones)
    return out.reshape(nt, RWKV_WIDTH)


def _merge_kernel(x_ref, ya_ref, yb_ref, yc_ref, wg0_ref, wg1_ref, wg2_ref, wa_ref, wb_ref, wc_ref, gb_ref, o_ref):
    x = x_ref[...]
    m = _sigmoid(_dot(x, wg0_ref[...]) + gb_ref[0:1, :]) * _dot(ya_ref[...], wa_ref[...])
    m = m + _sigmoid(_dot(x, wg1_ref[...]) + gb_ref[1:2, :]) * _dot(yb_ref[...], wb_ref[...])
    m = m + _sigmoid(_dot(x, wg2_ref[...]) + gb_ref[2:3, :]) * _dot(yc_ref[...], wc_ref[...])
    o_ref[...] = m.astype(o_ref.dtype)


def _merge(x_bf, ya, yb, yc, wg, gate_b, wa, wb, wc, layer, tm, tn):
    nt, d = x_bf.shape
    nj = d // tn
    act = lambda w: pl.BlockSpec((tm, w), lambda i, j: (i, 0))
    wcol = lambda k, off: pl.BlockSpec((None, k, tn), lambda i, j: (layer, 0, off + j))
    return pl.pallas_call(
        _merge_kernel,
        grid=(nt // tm, nj),
        in_specs=[act(d), act(SSD_WIDTH), act(S5_WIDTH), act(RWKV_WIDTH),
                  wcol(d, 0), wcol(d, nj), wcol(d, 2 * nj),
                  wcol(SSD_WIDTH, 0), wcol(S5_WIDTH, 0), wcol(RWKV_WIDTH, 0),
                  pl.BlockSpec((N_BRANCHES, tn), lambda i, j: (0, j))],
        out_specs=pl.BlockSpec((tm, tn), lambda i, j: (i, j)),
        out_shape=jax.ShapeDtypeStruct((nt, d), BF16),
        compiler_params=pltpu.CompilerParams(
            dimension_semantics=("parallel", "arbitrary"), vmem_limit_bytes=VMEM_LIMIT),
        name="gated_merge",
    )(x_bf, ya, yb, yc, wg, wg, wg, wa, wb, wc, gate_b)


def _out_kernel(m_ref, x_ref, wo_ref, g_ref, b_ref, xo_ref, xb_ref):
    h = DEEPNORM_ALPHA * x_ref[...] + _dot(m_ref[...], wo_ref[...])
    mu = jnp.mean(h, axis=-1, keepdims=True)
    hc = h - mu
    var = jnp.mean(hc * hc, axis=-1, keepdims=True)
    y = hc * lax.rsqrt(var + LN_EPS) * g_ref[...] + b_ref[...]
    xo_ref[...] = y
    xb_ref[...] = y.astype(BF16)


def _out_norm(merged, x_f32, wo, ln_g, ln_b, layer, tm):
    nt, d = x_f32.shape
    row = pl.BlockSpec((tm, d), lambda i: (i, 0))
    vec = pl.BlockSpec((1, d), lambda i: (0, 0))
    return pl.pallas_call(
        _out_kernel,
        grid=(nt // tm,),
        in_specs=[row, row, pl.BlockSpec((None, d, d), lambda i: (layer, 0, 0)), vec, vec],
        out_specs=[row, row],
        out_shape=[jax.ShapeDtypeStruct((nt, d), F32), jax.ShapeDtypeStruct((nt, d), BF16)],
        compiler_params=pltpu.CompilerParams(
            dimension_semantics=("parallel",), vmem_limit_bytes=VMEM_LIMIT),
        name="out_norm",
    )(merged, x_f32, wo, ln_g.reshape(1, d), ln_b.reshape(1, d))


_Tiles = collections.namedtuple("_Tiles", "tm tn_a tn_b tn_c tn_merge tm_out tb_ssd tb_s5 tb_rwkv npp_rwkv")


def _tiles(nt, seq):
    return _Tiles(tm=min(1024, nt), tn_a=5 * MXU_WIDTH, tn_b=4 * MXU_WIDTH, tn_c=6 * MXU_WIDTH,
                  tn_merge=MXU_WIDTH, tm_out=min(512, nt), tb_ssd=min(256, seq), tb_s5=min(256, seq),
                  tb_rwkv=min(RWKV_SUPER, seq), npp_rwkv=4)


def kernel(x, w_in, ssd_conv_w, ssd_conv_b, ssd_dt_bias, ssd_a_log, ssd_d, ssd_norm_w, s5_lambda_re, s5_lambda_im, s5_log_dt, s5_b_re, s5_b_im, s5_c_re, s5_c_im, s5_d, s5_w_glu, s5_b_glu, rwkv_mu, rwkv_mu_lora, rwkv_w0, rwkv_w2, rwkv_a0, rwkv_a2, rwkv_k_k, rwkv_k_a, rwkv_r_k, rwkv_lnx_w, rwkv_lnx_b, gate_b, w_branch_a, w_branch_b, w_branch_c, w_out, ln_g, ln_b):
    b, s, d = x.shape
    nt = b * s
    t = _tiles(nt, s)
    xf = x.reshape(nt, d)
    xb = xf.astype(BF16)
    w_a, w_b, w_c, w_g = _weight_slabs(w_in, 64)
    wb_a, wb_b, wb_c = w_branch_a.astype(BF16), w_branch_b.astype(BF16), w_branch_c.astype(BF16)
    w_o = w_out.astype(BF16)
    for l in range(DEPTH):
        pa = _project(xb, w_a, l, t.tm, t.tn_a)
        pb = _project(xb, w_b, l, t.tm, t.tn_b)
        pc = _project(xb, w_c, l, t.tm, t.tn_c)
        ya = _ssd_branch(pa, b, s, ssd_conv_w[l], ssd_conv_b[l], ssd_dt_bias[l], ssd_a_log[l], ssd_d[l],
                         ssd_norm_w[l], t.tb_ssd)
        yb = _s5_branch(pb, b, s, s5_lambda_re[l], s5_lambda_im[l], s5_log_dt[l], s5_b_re[l], s5_b_im[l],
                        s5_c_re[l], s5_c_im[l], s5_d[l], s5_w_glu[l], s5_b_glu[l], t.tb_s5)
        yc = _rwkv_branch(pc, b, s, rwkv_mu[l], rwkv_mu_lora[l], rwkv_w0[l], rwkv_w2[l], rwkv_a0[l], rwkv_a2[l],
                          rwkv_k_k[l], rwkv_k_a[l], rwkv_r_k[l], rwkv_lnx_w[l], rwkv_lnx_b[l], t.tb_rwkv,
                          t.npp_rwkv)
        merged = _merge(xb, ya, yb, yc, w_g, gate_b[l], wb_a, wb_b, wb_c, l, t.tm, t.tn_merge)
        xf, xb = _out_norm(merged, xf, w_o, ln_g[l], ln_b[l], l, t.tm_out)
    return xf.reshape(b, s, d)
```

```python
import collections
import functools

import jax
import jax.numpy as jnp
import numpy as np
from jax import lax
from jax.experimental import pallas as pl
from jax.experimental.pallas import tpu as pltpu

F32 = jnp.float32
BF16 = jnp.bfloat16
HIGHEST = lax.Precision.HIGHEST

LANES = 128
SUBLANES = 8
MXU_WIDTH = 256
VMEM_LIMIT = 56 * 1024 * 1024

D_MODEL = 2048
DEPTH = 4
SSD_HEADS = 32
SSD_HEAD_DIM = 64
SSD_WIDTH = SSD_HEADS * SSD_HEAD_DIM
SSD_GROUPS = 8
SSD_STATE = 128
SSD_CONV = 4
SSD_CHUNK = 128
SSD_NORM_EPS = 1e-5
SSD_GROUP_W = SSD_WIDTH // SSD_GROUPS
S5_GROUP_CH = 16
S5_GROUPS = 64
S5_WIDTH = S5_GROUPS * S5_GROUP_CH
S5_STATE = 64
S5_OCTETS = S5_WIDTH // LANES
S5_OCT_STATE = (LANES // S5_GROUP_CH) * S5_STATE
RWKV_HEADS = 16
RWKV_HEAD_DIM = 64
RWKV_WIDTH = RWKV_HEADS * RWKV_HEAD_DIM
RWKV_LORA = 64
RWKV_LNX_EPS = 64e-5
RWKV_CHUNK = 64
RWKV_SUPER = 256
N_BRANCHES = 3
DEEPNORM_ALPHA = (2 * DEPTH) ** 0.25
LN_EPS = 1e-5

OFF_A = 0
LEN_A = SSD_WIDTH + (SSD_WIDTH + 2 * SSD_GROUPS * SSD_STATE) + SSD_HEADS
OFF_B = OFF_A + LEN_A
LEN_B = 2 * S5_WIDTH
OFF_C = OFF_B + LEN_B
LEN_C = 4 * RWKV_WIDTH + 2 * RWKV_LORA
OFF_G = OFF_C + LEN_C
LEN_G = N_BRANCHES * D_MODEL
PAD_A = (-LEN_A) % LANES


def _silu(x):
    return x / (1.0 + jnp.exp(-x))


def _sigmoid(x):
    return 1.0 / (1.0 + jnp.exp(-x))


def _softplus(x):
    return jnp.maximum(x, 0.0) + jnp.log(1.0 + jnp.exp(-jnp.abs(x)))


def _dot(a, b):
    return jnp.dot(a, b, preferred_element_type=F32)


def _dot_nt(a, b):
    return lax.dot_general(a, b, (((1,), (1,)), ((), ())), preferred_element_type=F32)


def _dot_hilo(a, b_bf):
    hi = a.astype(BF16)
    lo = (a - hi.astype(F32)).astype(BF16)
    return _dot(hi, b_bf) + _dot(lo, b_bf)


def _shift_rows(x, carry8, d, row8):
    sh = pltpu.roll(x, d, 0)
    head = jnp.where(row8 < d, pltpu.roll(carry8, d, 0), sh[0:SUBLANES])
    return jnp.concatenate([head, sh[SUBLANES:]], axis=0)


def _mm_kernel(x_ref, wt_ref, o_ref):
    o_ref[...] = _dot_nt(x_ref[...], wt_ref[...]).astype(o_ref.dtype)


def _project(x_bf, wt_bf, layer, tm, tn):
    nt, d = x_bf.shape
    n = wt_bf.shape[1]
    return pl.pallas_call(
        _mm_kernel,
        grid=(nt // tm, pl.cdiv(n, tn)),
        in_specs=[pl.BlockSpec((tm, d), lambda i, j: (i, 0)),
                  pl.BlockSpec((None, tn, d), lambda i, j: (layer, j, 0))],
        out_specs=pl.BlockSpec((tm, tn), lambda i, j: (i, j)),
        out_shape=jax.ShapeDtypeStruct((nt, n), BF16),
        compiler_params=pltpu.CompilerParams(
            dimension_semantics=("parallel", "arbitrary"), vmem_limit_bytes=VMEM_LIMIT),
        name="in_proj",
    )(x_bf, wt_bf)


def _ssd_kernel(z_ref, xs_ref, bc_ref, dt_ref, cwx_ref, cbx_ref, cwbc_ref, cbbc_ref,
                dtb_ref, a_ref, dsk_ref, nw_ref, exp_ref, tri_ref,
                shift_ref, y_ref, state_ref, cx_ref, cbc_ref, *, tb):
    @pl.when(pl.program_id(1) == 0)
    def _():
        state_ref[...] = jnp.zeros_like(state_ref)
        cx_ref[...] = jnp.zeros_like(cx_ref)
        cbc_ref[...] = jnp.zeros_like(cbc_ref)

    row8 = lax.broadcasted_iota(jnp.int32, (SUBLANES, 1), 0)

    def conv(u_ref, carry_ref, w_ref, b_ref):
        u_bf = u_ref[...]
        shifted = _dot(shift_ref[...], u_bf)
        u = u_bf.astype(F32)
        c8 = carry_ref[...]
        acc = u * w_ref[SSD_CONV - 1:SSD_CONV, :] + b_ref[...]
        for d in range(1, SSD_CONV):
            sh = shifted[(d - 1) * tb:d * tb]
            head = jnp.where(row8 < d, pltpu.roll(c8, d, 0), sh[0:SUBLANES])
            sh = jnp.concatenate([head, sh[SUBLANES:]], axis=0)
            acc = acc + sh * w_ref[SSD_CONV - 1 - d:SSD_CONV - d, :]
        carry_ref[...] = u[tb - SUBLANES:tb]
        return _silu(acc)

    xs = conv(xs_ref, cx_ref, cwx_ref, cbx_ref)
    bc = conv(bc_ref, cbc_ref, cwbc_ref, cbbc_ref)

    L = SSD_CHUNK
    ti = lax.broadcasted_iota(jnp.int32, (L, L), 0)
    si = lax.broadcasted_iota(jnp.int32, (L, L), 1)
    causal = ti >= si
    lane = lax.broadcasted_iota(jnp.int32, (1, LANES), 1)
    lo_half = lane < SSD_HEAD_DIM
    tri = tri_ref[...]
    expand = exp_ref[...]
    gsn = SSD_GROUPS * SSD_STATE

    for ci in range(tb // L):
        r0 = ci * L
        dt = _softplus(dt_ref[r0:r0 + L, :].astype(F32) + dtb_ref[...])
        da = dt * a_ref[...]
        cs = jnp.dot(tri, da, precision=HIGHEST, preferred_element_type=F32)
        cs_t = cs.T
        dt_t = dt.T
        e_exp = _dot_hilo(jnp.exp(cs), expand)
        for g in range(SSD_GROUPS):
            c0 = g * SSD_GROUP_W
            x_g = xs[r0:r0 + L, c0:c0 + SSD_GROUP_W]
            b_g = bc[r0:r0 + L, g * SSD_STATE:(g + 1) * SSD_STATE]
            c_g = bc[r0:r0 + L, gsn + g * SSD_STATE:gsn + (g + 1) * SSD_STATE]
            c_bf = c_g.astype(BF16)
            cb = _dot_nt(c_bf, b_g.astype(BF16))
            b_t = b_g.T
            st_old = state_ref[:, c0:c0 + SSD_GROUP_W]
            y_off = _dot(c_bf, st_old.astype(BF16))
            y_parts, new_parts = [], []
            for pr in range(2):
                xp = x_g[:, pr * LANES:(pr + 1) * LANES]
                x_half = (jnp.where(lo_half, xp, 0.0).astype(BF16), jnp.where(lo_half, 0.0, xp).astype(BF16))
                y_p = None
                n_p = None
                for k in range(2):
                    h = g * 4 + pr * 2 + k
                    col = cs[:, h:h + 1]
                    row = cs_t[h:h + 1, :]
                    dtrow = dt_t[h:h + 1, :]
                    lmat = jnp.exp(jnp.where(causal, col - row, -1e30))
                    m_h = (cb * lmat * dtrow).astype(BF16)
                    y_k = _dot(m_h, x_half[k])
                    wrow = jnp.exp(cs_t[h:h + 1, L - 1:L] - row) * dtrow
                    n_k = _dot((b_t * wrow).astype(BF16), x_half[k])
                    y_p = y_k if y_p is None else y_p + y_k
                    n_p = n_k if n_p is None else n_p + n_k
                y_parts.append(y_p)
                new_parts.append(n_p)
            y_diag = jnp.concatenate(y_parts, axis=1)
            new_g = jnp.concatenate(new_parts, axis=1)
            e_g = e_exp[:, c0:c0 + SSD_GROUP_W]
            state_ref[:, c0:c0 + SSD_GROUP_W] = st_old * e_g[L - 1:L, :] + new_g
            y = y_diag + y_off * e_g + dsk_ref[:, c0:c0 + SSD_GROUP_W] * x_g
            y = y * _silu(z_ref[r0:r0 + L, c0:c0 + SSD_GROUP_W].astype(F32))
            ms = jnp.mean(y * y, axis=-1, keepdims=True)
            y = y * lax.rsqrt(ms + SSD_NORM_EPS) * nw_ref[:, c0:c0 + SSD_GROUP_W]
            y_ref[r0:r0 + L, c0:c0 + SSD_GROUP_W] = y.astype(y_ref.dtype)


def _ssd_branch(proj_a, batch, seq, conv_w, conv_b, dt_bias, a_log, d_skip, norm_w, tb):
    nt = batch * seq
    nblk = seq // tb
    w2 = SSD_WIDTH // D_MODEL
    assert w2 == 1
    dt_blk = (2 * SSD_WIDTH + 2 * SSD_GROUPS * SSD_STATE) // LANES
    pad_h = LANES - SSD_HEADS
    dtb = jnp.pad(dt_bias, (0, pad_h)).reshape(1, LANES)
    a_neg = jnp.pad(-jnp.exp(a_log.astype(F32)), (0, pad_h)).reshape(1, LANES)
    dsk = jnp.repeat(d_skip, SSD_HEAD_DIM).reshape(1, SSD_WIDTH)
    expand = (jnp.arange(LANES)[:, None] == (jnp.arange(SSD_WIDTH)[None, :] // SSD_HEAD_DIM)).astype(BF16)
    tri = jnp.tril(jnp.ones((SSD_CHUNK, SSD_CHUNK), F32))
    t_idx = jnp.arange(tb)
    shift = jnp.concatenate([(t_idx[None, :] == t_idx[:, None] - d).astype(BF16) for d in range(1, SSD_CONV)],
                            axis=0)
    row = lambda b, i: (b * nblk + i, 0)
    full = lambda b, i: (0, 0)
    return pl.pallas_call(
        functools.partial(_ssd_kernel, tb=tb),
        grid=(batch, nblk),
        in_specs=[
            pl.BlockSpec((tb, SSD_WIDTH), lambda b, i: (b * nblk + i, 0)),
            pl.BlockSpec((tb, SSD_WIDTH), lambda b, i: (b * nblk + i, 1)),
            pl.BlockSpec((tb, SSD_WIDTH), lambda b, i: (b * nblk + i, 2)),
            pl.BlockSpec((tb, LANES), lambda b, i: (b * nblk + i, dt_blk)),
            pl.BlockSpec((SSD_CONV, SSD_WIDTH), lambda b, i: (0, 0)),
            pl.BlockSpec((1, SSD_WIDTH), lambda b, i: (0, 0)),
            pl.BlockSpec((SSD_CONV, SSD_WIDTH), lambda b, i: (0, 1)),
            pl.BlockSpec((1, SSD_WIDTH), lambda b, i: (0, 1)),
            pl.BlockSpec((1, LANES), full), pl.BlockSpec((1, LANES), full),
            pl.BlockSpec((1, SSD_WIDTH), full), pl.BlockSpec((1, SSD_WIDTH), full),
            pl.BlockSpec((LANES, SSD_WIDTH), full), pl.BlockSpec((SSD_CHUNK, SSD_CHUNK), full),
            pl.BlockSpec(((SSD_CONV - 1) * tb, tb), full),
        ],
        out_specs=pl.BlockSpec((tb, SSD_WIDTH), row),
        out_shape=jax.ShapeDtypeStruct((nt, SSD_WIDTH), BF16),
        scratch_shapes=[pltpu.VMEM((SSD_STATE, SSD_WIDTH), F32),
                        pltpu.VMEM((SUBLANES, SSD_WIDTH), F32),
                        pltpu.VMEM((SUBLANES, SSD_WIDTH), F32)],
        compiler_params=pltpu.CompilerParams(
            dimension_semantics=("parallel", "arbitrary"), vmem_limit_bytes=VMEM_LIMIT),
        name="ssd_branch",
    )(proj_a, proj_a, proj_a, proj_a, conv_w, conv_b.reshape(1, -1), conv_w, conv_b.reshape(1, -1),
      dtb, a_neg, dsk, norm_w.reshape(1, -1), expand, tri, shift)


S5_QUADS = 4
S5_BLOCKS = 2 * S5_GROUPS * S5_STATE // LANES
S5_QBLK = S5_BLOCKS // S5_QUADS


def _s5_kernel(u_ref, z_ref, bb_ref, co_ref, lr_ref, li_ref, d_ref, wglu_ref, bglu_ref,
               y_ref, scr_ref, st_ref, *, tb, tpad, nbatch):
    @pl.when(pl.program_id(0) == 0)
    def _():
        st_ref[...] = jnp.zeros_like(st_ref)

    half_blk = S5_OCT_STATE // LANES
    nvec = 2 * S5_QUADS
    for b in range(nbatch):
        u_bf = u_ref[b]
        for o in range(S5_OCTETS):
            bu = _dot(u_bf[:, o * LANES:(o + 1) * LANES], bb_ref[o])
            q, half = divmod(o, 2)
            for j in range(half_blk):
                jr = S5_QBLK * q + half_blk * half + j
                ji = jr + SUBLANES
                scr_ref[b, jr * tpad:jr * tpad + tb, :] = bu[:, j * LANES:(j + 1) * LANES]
                scr_ref[b, ji * tpad:ji * tpad + tb, :] = bu[:, S5_OCT_STATE + j * LANES:
                                                              S5_OCT_STATE + (j + 1) * LANES]

    lr = [lr_ref[q] for q in range(S5_QUADS)]
    li = [li_ref[q] for q in range(S5_QUADS)]

    def step(t, carry):
        new = []
        for b in range(nbatch):
            for q in range(S5_QUADS):
                sr, si = carry[nvec * b + 2 * q], carry[nvec * b + 2 * q + 1]
                re_rows = pl.ds(S5_QBLK * q * tpad + t, SUBLANES, stride=tpad)
                im_rows = pl.ds((S5_QBLK * q + SUBLANES) * tpad + t, SUBLANES, stride=tpad)
                nr = lr[q] * sr - li[q] * si + scr_ref[b, re_rows, :]
                ni = lr[q] * si + li[q] * sr + scr_ref[b, im_rows, :]
                scr_ref[b, re_rows, :] = nr
                scr_ref[b, im_rows, :] = ni
                new += [nr, ni]
        return tuple(new)

    carry = lax.fori_loop(0, tb, step, tuple(st_ref[i] for i in range(nbatch * nvec)), unroll=8)
    for i in range(nbatch * nvec):
        st_ref[i] = carry[i]

    for b in range(nbatch):
        ys = []
        for o in range(S5_OCTETS):
            q, half = divmod(o, 2)
            blks = []
            for part in range(2):
                for j in range(half_blk):
                    jb = S5_QBLK * q + SUBLANES * part + half_blk * half + j
                    blks.append(scr_ref[b, jb * tpad:jb * tpad + tb, :].astype(BF16))
            ys.append(_dot(jnp.concatenate(blks, axis=1), co_ref[o]))
        y = jnp.concatenate(ys, axis=1) + d_ref[...] * u_ref[b].astype(F32)
        y = 0.5 * y * (1.0 + jnp.tanh(0.7978845608028654 * (y + 0.044715 * (y * y * y))))
        gl = _dot(y.astype(BF16), wglu_ref[...]) + bglu_ref[...]
        y = y * _sigmoid(gl)
        y_ref[b] = (y * _silu(z_ref[b].astype(F32))).astype(y_ref.dtype)


def _s5_branch(proj_b, batch, seq, lam_re, lam_im, log_dt, b_re, b_im, c_re, c_im, d_s5, w_glu, b_glu, tb):
    nt = batch * seq
    nblk = seq // tb
    tpad = tb + SUBLANES
    dt = jnp.exp(log_dt)[:, None]
    mag = jnp.exp(lam_re * dt)
    ang = lam_im * dt
    lb_re, lb_im = mag * jnp.cos(ang), mag * jnp.sin(ang)
    den = jnp.square(lam_re) + jnp.square(lam_im)
    nr, ni = lb_re - 1.0, lb_im
    q_re = (nr * lam_re + ni * lam_im) / den
    q_im = (ni * lam_re - nr * lam_im) / den
    bb_re = q_re[..., None] * b_re - q_im[..., None] * b_im
    bb_im = q_re[..., None] * b_im + q_im[..., None] * b_re
    g8 = LANES // S5_GROUP_CH
    eye = jnp.eye(g8, dtype=F32)

    def in_mat(bb):
        m = jnp.einsum('ogph,gk->oghkp', bb.reshape(S5_OCTETS, g8, S5_STATE, S5_GROUP_CH), eye)
        return m.reshape(S5_OCTETS, LANES, S5_OCT_STATE)

    def out_mat(c):
        m = jnp.einsum('oghp,gk->okpgh', c.reshape(S5_OCTETS, g8, S5_GROUP_CH, S5_STATE), eye)
        return m.reshape(S5_OCTETS, S5_OCT_STATE, LANES)

    bb = jnp.concatenate([in_mat(bb_re), in_mat(bb_im)], axis=2).astype(BF16)
    co = jnp.concatenate([out_mat(c_re), -out_mat(c_im)], axis=1).astype(BF16)
    lr = lb_re.reshape(S5_QUADS, SUBLANES, LANES)
    li = lb_im.reshape(S5_QUADS, SUBLANES, LANES)
    full2 = lambda i: (0, 0)
    full3 = lambda i: (0, 0, 0)
    proj3 = proj_b.reshape(batch, seq, proj_b.shape[1])
    out = pl.pallas_call(
        functools.partial(_s5_kernel, tb=tb, tpad=tpad, nbatch=batch),
        grid=(nblk,),
        in_specs=[
            pl.BlockSpec((batch, tb, S5_WIDTH), lambda i: (0, i, 0)),
            pl.BlockSpec((batch, tb, S5_WIDTH), lambda i: (0, i, 1)),
            pl.BlockSpec((S5_OCTETS, LANES, 2 * S5_OCT_STATE), full3),
            pl.BlockSpec((S5_OCTETS, 2 * S5_OCT_STATE, LANES), full3),
            pl.BlockSpec((S5_QUADS, SUBLANES, LANES), full3),
            pl.BlockSpec((S5_QUADS, SUBLANES, LANES), full3),
            pl.BlockSpec((1, S5_WIDTH), full2),
            pl.BlockSpec((S5_WIDTH, S5_WIDTH), full2),
            pl.BlockSpec((1, S5_WIDTH), full2),
        ],
        out_specs=pl.BlockSpec((batch, tb, S5_WIDTH), lambda i: (0, i, 0)),
        out_shape=jax.ShapeDtypeStruct((batch, seq, S5_WIDTH), BF16),
        scratch_shapes=[pltpu.VMEM((batch, S5_BLOCKS * tpad, LANES), F32),
                        pltpu.VMEM((batch * 2 * S5_QUADS, SUBLANES, LANES), F32)],
        compiler_params=pltpu.CompilerParams(
            dimension_semantics=("arbitrary",), vmem_limit_bytes=VMEM_LIMIT),
        name="s5_branch",
    )(proj3, proj3, bb, co, lr, li, d_s5.reshape(1, -1), w_glu.astype(BF16), b_glu.reshape(1, -1))
    return out.reshape(nt, S5_WIDTH)


def _dot_rhs_hilo(m_bf, x):
    hi = x.astype(BF16)
    lo = (x - hi.astype(F32)).astype(BF16)
    return _dot(m_bf, hi) + _dot(m_bf, lo)


def _rwkv_kernel(r_ref, k_ref, v_ref, z_ref, lo_ref, mu_ref, mulo_ref, wl_ref, w0_ref, a0_ref, kk_ref, ka_ref,
                 rk_ref, lw_ref, lb_ref, ones_ref, tri_ref, cones_ref, y_ref, st_ref, carry_ref, *, tb, nbatch,
                 npp):
    @pl.when(pl.program_id(1) == 0)
    def _():
        st_ref[...] = jnp.zeros_like(st_ref)
        carry_ref[...] = jnp.zeros_like(carry_ref)

    L = RWKV_CHUNK
    hd = RWKV_HEAD_DIM
    row8 = lax.broadcasted_iota(jnp.int32, (SUBLANES, 1), 0)
    lane = lax.broadcasted_iota(jnp.int32, (1, LANES), 1)
    head_mask = (lane < hd, lane >= hd)
    ones = ones_ref[...]

    def lerp(cur, cslot, mu):
        prev = _shift_rows(cur, carry_ref[cslot], 1, row8)
        carry_ref[cslot] = cur[tb - SUBLANES:tb]
        return cur + (prev - cur) * mu

    seqs = [(b, p) for b in range(nbatch) for p in range(npp)]
    sq = []
    for b in range(nbatch):
        xl = lerp(lo_ref[b].astype(F32), (b, 4, 0), mulo_ref[...])
        lin = jnp.where(head_mask[0], jnp.tanh(xl), xl).astype(BF16)
        for p in range(npp):
            ls = slice(p * LANES, (p + 1) * LANES)
            r = lerp(r_ref[b, :, ls].astype(F32), (b, 0, p), mu_ref[0:1, ls])
            k = lerp(k_ref[b, :, ls].astype(F32), (b, 1, p), mu_ref[1:2, ls])
            v = lerp(v_ref[b, :, ls].astype(F32), (b, 2, p), mu_ref[2:3, ls])
            zc = lerp(z_ref[b, :, ls].astype(F32), (b, 3, p), mu_ref[3:4, ls])
            wa = _dot(lin, wl_ref[ls, :])
            w_log = -_softplus(-(w0_ref[:, ls] + wa[:, :LANES])) - 0.5
            a = _sigmoid(a0_ref[:, ls] + wa[:, LANES:])
            kkr = k * kk_ref[:, ls]
            kk = kkr * lax.rsqrt(jnp.maximum(_dot((kkr * kkr).astype(BF16), ones), 1e-24))
            sq.append(dict(r=r, v=v, zc=zc, k2=k * (1.0 + (a - 1.0) * ka_ref[:, ls]), av=-kk, bv=kk * a,
                           logdec=-jnp.exp(w_log)))

    sbr = RWKV_SUPER
    nsb = tb // sbr
    ti = lax.broadcasted_iota(jnp.int32, (sbr, sbr), 0)
    si = lax.broadcasted_iota(jnp.int32, (sbr, sbr), 1)
    same_chunk = (ti // L) == (si // L)
    strict = same_chunk & (si < ti)
    incl = same_chunk & (si <= ti)
    ri = lax.broadcasted_iota(jnp.int32, (LANES, LANES), 0)
    cj = lax.broadcasted_iota(jnp.int32, (LANES, LANES), 1)
    blockdiag2 = jnp.concatenate([(ri // hd) == (cj // hd)] * 2, axis=1)
    eye = ri == cj
    tri = tri_ref[...]
    cones = cones_ref[...]
    zeros_l = jnp.zeros((L, LANES), F32)

    units = [(s, sb) for s in range(len(seqs)) for sb in range(nsb)]
    pre = {}
    for u in units:
        s, sb = u
        q = sq[s]
        sl = slice(sb * sbr, (sb + 1) * sbr)
        ld = q['logdec'][sl]
        cum = _dot_rhs_hilo(tri, ld)
        ctot = _dot_rhs_hilo(cones, ld)
        e_neg = jnp.exp(-cum)
        e_end = jnp.exp(ctot - cum)
        pre[u] = dict(rt=q['r'][sl] * jnp.exp(cum), at=q['av'][sl] * jnp.exp(cum - ld), kt=q['k2'][sl] * e_neg,
                      bt=q['bv'][sl] * e_neg, kh=q['k2'][sl] * e_end, bh=q['bv'][sl] * e_end, v=q['v'][sl],
                      ctot=ctot)

    chains = [(u, h) for u in units for h in range(2)]
    xs, ps, arb, ark, vh = {}, {}, {}, {}, {}
    for c in chains:
        sb, h = c
        d = pre[sb]
        m = head_mask[h]
        at_h = jnp.where(m, d['at'], 0.0)
        lhs = jnp.concatenate([at_h, jnp.where(m, d['rt'], 0.0)], axis=0).astype(BF16)
        rhs1 = jnp.concatenate([d['bt'], d['kt']], axis=0).astype(BF16)
        a4 = _dot_nt(lhs, rhs1)
        ps[c] = jnp.where(strict, a4[0:sbr, 0:sbr], 0.0)
        aak = jnp.where(strict, a4[0:sbr, sbr:], 0.0).astype(BF16)
        arb[c] = jnp.where(incl, a4[sbr:, 0:sbr], 0.0).astype(BF16)
        ark[c] = jnp.where(incl, a4[sbr:, sbr:], 0.0).astype(BF16)
        vh[c] = jnp.where(m, d['v'], 0.0).astype(BF16)
        xs[c] = jnp.concatenate([at_h, _dot(aak, vh[c])], axis=1)
    n_dbl = L.bit_length() - 1
    for i in range(n_dbl):
        for c in chains:
            p_bf = ps[c].astype(BF16)
            xs[c] = xs[c] + _dot(p_bf, xs[c].astype(BF16))
            if i + 1 < n_dbl:
                ps[c] = _dot(p_bf, p_bf)
    w_sb, u_sb, q_sb, y0_sb = {}, {}, {}, {}
    for sb in units:
        w_s = xs[(sb, 0)][:, :LANES] + xs[(sb, 1)][:, :LANES]
        u_s = xs[(sb, 0)][:, LANES:] + xs[(sb, 1)][:, LANES:]
        q_s = pre[sb]['rt']
        y0_s = None
        for h in range(2):
            c = (sb, h)
            qy = _dot(arb[c], xs[c].astype(BF16))
            y0_h = qy[:, LANES:] + _dot(ark[c], vh[c])
            q_s = q_s + qy[:, :LANES]
            y0_s = y0_h if y0_s is None else y0_s + y0_h
        w_sb[sb] = w_s
        u_sb[sb] = u_s
        q_sb[sb] = q_s
        y0_sb[sb] = y0_s

    nchunk = tb // L
    mns, gcols = {}, {}
    for ci in range(nchunk):
        sbi, lo_r = divmod(ci * L, sbr)
        cs = slice(lo_r, lo_r + L)
        for s in range(len(seqs)):
            u = (s, sbi)
            d = pre[u]
            rhs2 = jnp.concatenate([jnp.concatenate([w_sb[u][cs], u_sb[u][cs]], axis=1),
                                    jnp.concatenate([zeros_l, d['v'][cs]], axis=1)], axis=0).astype(BF16)
            lhs3 = jnp.concatenate([d['bh'][cs], d['kh'][cs]], axis=0).T.astype(BF16)
            mns[(s, ci)] = jnp.where(blockdiag2, _dot(lhs3, rhs2), 0.0)
            gl = jnp.exp(d['ctot'][lo_r:lo_r + 1, :])
            gcols[(s, ci)] = jnp.sum(jnp.where(eye, gl, 0.0), axis=1, keepdims=True)

    states = [st_ref[s] for s in range(len(seqs))]
    ys = [[] for _ in seqs]
    for ci in range(nchunk):
        sbi, lo_r = divmod(ci * L, sbr)
        cs = slice(lo_r, lo_r + L)
        for s in range(len(seqs)):
            u = (s, sbi)
            st = states[s]
            ys[s].append(_dot(q_sb[u][cs].astype(BF16), st.astype(BF16)) + y0_sb[u][cs])
            mn = mns[(s, ci)]
            states[s] = gcols[(s, ci)] * st + _dot_rhs_hilo(mn[:, :LANES].astype(BF16), st) + mn[:, LANES:]

    inv_hd = 1.0 / hd
    for s, (b, p) in enumerate(seqs):
        st_ref[s] = states[s]
        q = sq[s]
        ls = slice(p * LANES, (p + 1) * LANES)
        y = jnp.concatenate(ys[s], axis=0)
        mean = _dot_hilo(y, ones) * inv_hd
        yc = y - mean
        var = _dot((yc * yc).astype(BF16), ones) * inv_hd
        yn = yc * lax.rsqrt(var + RWKV_LNX_EPS) * lw_ref[:, ls] + lb_ref[:, ls]
        rk = _dot((q['r'] * q['k2'] * rk_ref[:, ls]).astype(BF16), ones)
        y_ref[b, :, ls] = ((yn + rk * q['v']) * _silu(q['zc'])).astype(y_ref.dtype)


def _rwkv_branch(proj_c, batch, seq, mu, mu_lora, w0, w2, a0, a2, k_k, k_a, r_k, lnx_w, lnx_b, tb, npp):
    nt = batch * seq
    nblk = seq // tb
    npair = RWKV_WIDTH // LANES
    wblk = npp * LANES
    cb = RWKV_WIDTH // wblk
    proj3 = proj_c.reshape(batch, seq, proj_c.shape[1])
    w2p = w2.reshape(RWKV_LORA, npair, LANES).transpose(1, 0, 2)
    a2p = a2.reshape(RWKV_LORA, npair, LANES).transpose(1, 0, 2)
    zer = jnp.zeros_like(w2p)
    wl = jnp.concatenate([jnp.concatenate([w2p, zer], axis=2), jnp.concatenate([zer, a2p], axis=2)], axis=1)
    wl = wl.reshape(npair * LANES, 2 * LANES).astype(BF16)
    ones = (jnp.arange(LANES)[:, None] // RWKV_HEAD_DIM == jnp.arange(LANES)[None, :] // RWKV_HEAD_DIM).astype(BF16)
    idx = jnp.arange(RWKV_SUPER)
    same_chunk = idx[:, None] // RWKV_CHUNK == idx[None, :] // RWKV_CHUNK
    cones = same_chunk.astype(BF16)
    tri = (same_chunk & (idx[None, :] <= idx[:, None])).astype(BF16)
    vec = lambda p: p.reshape(1, RWKV_WIDTH)
    tok = lambda off: pl.BlockSpec((batch, tb, wblk), lambda h, i: (0, i, off + h))
    pvec = pl.BlockSpec((1, wblk), lambda h, i: (0, h))
    full = lambda h, i: (0, 0)
    out = pl.pallas_call(
        functools.partial(_rwkv_kernel, tb=tb, nbatch=batch, npp=npp),
        grid=(npair // npp, nblk),
        in_specs=[
            tok(0), tok(cb), tok(2 * cb), tok(3 * cb),
            pl.BlockSpec((batch, tb, LANES), lambda h, i: (0, i, 4 * RWKV_WIDTH // LANES)),
            pl.BlockSpec((4, wblk), lambda h, i: (0, h)),
            pl.BlockSpec((1, LANES), full),
            pl.BlockSpec((wblk, 2 * LANES), lambda h, i: (h, 0)),
            pvec, pvec, pvec, pvec, pvec, pvec, pvec,
            pl.BlockSpec((LANES, LANES), full),
            pl.BlockSpec((RWKV_SUPER, RWKV_SUPER), full),
            pl.BlockSpec((RWKV_SUPER, RWKV_SUPER), full),
        ],
        out_specs=pl.BlockSpec((batch, tb, wblk), lambda h, i: (0, i, h)),
        out_shape=jax.ShapeDtypeStruct((batch, seq, RWKV_WIDTH), BF16),
        scratch_shapes=[pltpu.VMEM((batch * npp, LANES, LANES), F32),
                        pltpu.VMEM((batch, 5, npp, SUBLANES, LANES), F32)],
        compiler_params=pltpu.CompilerParams(
            dimension_semantics=("parallel", "arbitrary"), vmem_limit_bytes=VMEM_LIMIT),
        name="rwkv_branch",
    )(proj3, proj3, proj3, proj3, proj3, mu, mu_lora.reshape(1, 2 * RWKV_LORA), wl, vec(w0), vec(a0),
      vec(k_k), vec(k_a), vec(r_k), vec(lnx_w), vec(lnx_b), ones, tri, cones)
    return out.reshape(nt, RWKV_WIDTH)


def _merge_kernel(x_ref, ya_ref, yb_ref, yc_ref, wg0_ref, wg1_ref, wg2_ref, wa_ref, wb_ref, wc_ref, gb_ref, o_ref):
    x = x_ref[...]
    m = _sigmoid(_dot_nt(x, wg0_ref[...]) + gb_ref[0:1, :]) * _dot(ya_ref[...], wa_ref[...])
    m = m + _sigmoid(_dot_nt(x, wg1_ref[...]) + gb_ref[1:2, :]) * _dot(yb_ref[...], wb_ref[...])
    m = m + _sigmoid(_dot_nt(x, wg2_ref[...]) + gb_ref[2:3, :]) * _dot(yc_ref[...], wc_ref[...])
    o_ref[...] = m.astype(o_ref.dtype)


def _merge(x_bf, ya, yb, yc, wg, gate_b, wa, wb, wc, layer, tm, tn):
    nt, d = x_bf.shape
    nj = d // tn
    act = lambda w: pl.BlockSpec((tm, w), lambda i, j: (i, 0))
    wcol = lambda k, off: pl.BlockSpec((None, k, tn), lambda i, j: (layer, 0, off + j))
    wrow = lambda off: pl.BlockSpec((None, tn, d), lambda i, j: (layer, off + j, 0))
    return pl.pallas_call(
        _merge_kernel,
        grid=(nt // tm, nj),
        in_specs=[act(d), act(SSD_WIDTH), act(S5_WIDTH), act(RWKV_WIDTH),
                  wrow(0), wrow(nj), wrow(2 * nj),
                  wcol(SSD_WIDTH, 0), wcol(S5_WIDTH, 0), wcol(RWKV_WIDTH, 0),
                  pl.BlockSpec((N_BRANCHES, tn), lambda i, j: (0, j))],
        out_specs=pl.BlockSpec((tm, tn), lambda i, j: (i, j)),
        out_shape=jax.ShapeDtypeStruct((nt, d), BF16),
        compiler_params=pltpu.CompilerParams(
            dimension_semantics=("parallel", "arbitrary"), vmem_limit_bytes=VMEM_LIMIT),
        name="gated_merge",
    )(x_bf, ya, yb, yc, wg, wg, wg, wa, wb, wc, gate_b)


def _out_kernel(m_ref, x_ref, wo_ref, g_ref, b_ref, xo_ref, xb_ref):
    h = DEEPNORM_ALPHA * x_ref[...] + _dot(m_ref[...], wo_ref[...])
    mu = jnp.mean(h, axis=-1, keepdims=True)
    hc = h - mu
    var = jnp.mean(hc * hc, axis=-1, keepdims=True)
    y = hc * lax.rsqrt(var + LN_EPS) * g_ref[...] + b_ref[...]
    xo_ref[...] = y
    xb_ref[...] = y.astype(BF16)


def _out_norm(merged, x_f32, wo, ln_g, ln_b, layer, tm):
    nt, d = x_f32.shape
    row = pl.BlockSpec((tm, d), lambda i: (i, 0))
    vec = pl.BlockSpec((1, d), lambda i: (0, 0))
    return pl.pallas_call(
        _out_kernel,
        grid=(nt // tm,),
        in_specs=[row, row, pl.BlockSpec((None, d, d), lambda i: (layer, 0, 0)), vec, vec],
        out_specs=[row, row],
        out_shape=[jax.ShapeDtypeStruct((nt, d), F32), jax.ShapeDtypeStruct((nt, d), BF16)],
        compiler_params=pltpu.CompilerParams(
            dimension_semantics=("parallel",), vmem_limit_bytes=VMEM_LIMIT),
        name="out_norm",
    )(merged, x_f32, wo, ln_g.reshape(1, d), ln_b.reshape(1, d))


_Tiles = collections.namedtuple("_Tiles", "tm tn_a tn_b tn_c tn_merge tm_out tb_ssd tb_s5 tb_rwkv npp_rwkv")


def _tiles(nt, seq):
    return _Tiles(tm=min(1024, nt), tn_a=5 * MXU_WIDTH, tn_b=4 * MXU_WIDTH, tn_c=6 * MXU_WIDTH,
                  tn_merge=MXU_WIDTH, tm_out=min(512, nt), tb_ssd=min(256, seq), tb_s5=min(256, seq),
                  tb_rwkv=min(RWKV_SUPER, seq), npp_rwkv=4)


def kernel(x, w_in, ssd_conv_w, ssd_conv_b, ssd_dt_bias, ssd_a_log, ssd_d, ssd_norm_w, s5_lambda_re, s5_lambda_im, s5_log_dt, s5_b_re, s5_b_im, s5_c_re, s5_c_im, s5_d, s5_w_glu, s5_b_glu, rwkv_mu, rwkv_mu_lora, rwkv_w0, rwkv_w2, rwkv_a0, rwkv_a2, rwkv_k_k, rwkv_k_a, rwkv_r_k, rwkv_lnx_w, rwkv_lnx_b, gate_b, w_branch_a, w_branch_b, w_branch_c, w_out, ln_g, ln_b):
    b, s, d = x.shape
    nt = b * s
    t = _tiles(nt, s)
    xf = x.reshape(nt, d)
    xb = xf.astype(BF16)
    wt = jnp.swapaxes(w_in, 1, 2).astype(BF16)
    w_a = wt[:, OFF_A:OFF_A + LEN_A + PAD_A]
    w_b = wt[:, OFF_B:OFF_B + LEN_B]
    w_c = wt[:, OFF_C:OFF_C + LEN_C]
    w_g = wt[:, OFF_G:OFF_G + LEN_G]
    wb_a, wb_b, wb_c = w_branch_a.astype(BF16), w_branch_b.astype(BF16), w_branch_c.astype(BF16)
    w_o = w_out.astype(BF16)
    for l in range(DEPTH):
        pa = _project(xb, w_a, l, t.tm, t.tn_a)
        pb = _project(xb, w_b, l, t.tm, t.tn_b)
        pc = _project(xb, w_c, l, t.tm, t.tn_c)
        ya = _ssd_branch(pa, b, s, ssd_conv_w[l], ssd_conv_b[l], ssd_dt_bias[l], ssd_a_log[l], ssd_d[l],
                         ssd_norm_w[l], t.tb_ssd)
        yb = _s5_branch(pb, b, s, s5_lambda_re[l], s5_lambda_im[l], s5_log_dt[l], s5_b_re[l], s5_b_im[l],
                        s5_c_re[l], s5_c_im[l], s5_d[l], s5_w_glu[l], s5_b_glu[l], t.tb_s5)
        yc = _rwkv_branch(pc, b, s, rwkv_mu[l], rwkv_mu_lora[l], rwkv_w0[l], rwkv_w2[l], rwkv_a0[l], rwkv_a2[l],
                          rwkv_k_k[l], rwkv_k_a[l], rwkv_r_k[l], rwkv_lnx_w[l], rwkv_lnx_b[l], t.tb_rwkv,
                          t.npp_rwkv)
        merged = _merge(xb, ya, yb, yc, w_g, gate_b[l], wb_a, wb_b, wb_c, l, t.tm, t.tn_merge)
        xf, xb = _out_norm(merged, xf, w_o, ln_g[l], ln_b[l], l, t.tm_out)
    return xf.reshape(b, s, d)
```

```python
import collections
import functools

import jax
import jax.numpy as jnp
import numpy as np
from jax import lax
from jax.experimental import pallas as pl
from jax.experimental.pallas import tpu as pltpu

F32 = jnp.float32
BF16 = jnp.bfloat16
HIGHEST = lax.Precision.HIGHEST

LANES = 128
SUBLANES = 8
MXU_WIDTH = 256
VMEM_LIMIT = 56 * 1024 * 1024

D_MODEL = 2048
DEPTH = 4
SSD_HEADS = 32
SSD_HEAD_DIM = 64
SSD_WIDTH = SSD_HEADS * SSD_HEAD_DIM
SSD_GROUPS = 8
SSD_STATE = 128
SSD_CONV = 4
SSD_CHUNK = 128
SSD_NORM_EPS = 1e-5
SSD_GROUP_W = SSD_WIDTH // SSD_GROUPS
S5_GROUP_CH = 16
S5_GROUPS = 64
S5_WIDTH = S5_GROUPS * S5_GROUP_CH
S5_STATE = 64
S5_OCTETS = S5_WIDTH // LANES
S5_OCT_STATE = (LANES // S5_GROUP_CH) * S5_STATE
RWKV_HEADS = 16
RWKV_HEAD_DIM = 64
RWKV_WIDTH = RWKV_HEADS * RWKV_HEAD_DIM
RWKV_LORA = 64
RWKV_LNX_EPS = 64e-5
RWKV_CHUNK = 64
RWKV_SUPER = 256
N_BRANCHES = 3
DEEPNORM_ALPHA = (2 * DEPTH) ** 0.25
LN_EPS = 1e-5

OFF_A = 0
LEN_A = SSD_WIDTH + (SSD_WIDTH + 2 * SSD_GROUPS * SSD_STATE) + SSD_HEADS
OFF_B = OFF_A + LEN_A
LEN_B = 2 * S5_WIDTH
OFF_C = OFF_B + LEN_B
LEN_C = 4 * RWKV_WIDTH + 2 * RWKV_LORA
OFF_G = OFF_C + LEN_C
LEN_G = N_BRANCHES * D_MODEL
PAD_A = (-LEN_A) % LANES


def _silu(x):
    return x / (1.0 + jnp.exp(-x))


def _sigmoid(x):
    return 1.0 / (1.0 + jnp.exp(-x))


def _softplus(x):
    return jnp.maximum(x, 0.0) + jnp.log(1.0 + jnp.exp(-jnp.abs(x)))


def _dot(a, b):
    return jnp.dot(a, b, preferred_element_type=F32)


def _dot_nt(a, b):
    return lax.dot_general(a, b, (((1,), (1,)), ((), ())), preferred_element_type=F32)


def _dot_hilo(a, b_bf):
    hi = a.astype(BF16)
    lo = (a - hi.astype(F32)).astype(BF16)
    return _dot(hi, b_bf) + _dot(lo, b_bf)


def _shift_rows(x, carry8, d, row8):
    sh = pltpu.roll(x, d, 0)
    head = jnp.where(row8 < d, pltpu.roll(carry8, d, 0), sh[0:SUBLANES])
    return jnp.concatenate([head, sh[SUBLANES:]], axis=0)


def _mm_kernel(x_ref, wt_ref, o_ref):
    o_ref[...] = _dot_nt(x_ref[...], wt_ref[...]).astype(o_ref.dtype)


def _project(x_bf, wt_bf, layer, tm, tn):
    nt, d = x_bf.shape
    n = wt_bf.shape[1]
    return pl.pallas_call(
        _mm_kernel,
        grid=(nt // tm, pl.cdiv(n, tn)),
        in_specs=[pl.BlockSpec((tm, d), lambda i, j: (i, 0)),
                  pl.BlockSpec((None, tn, d), lambda i, j: (layer, j, 0))],
        out_specs=pl.BlockSpec((tm, tn), lambda i, j: (i, j)),
        out_shape=jax.ShapeDtypeStruct((nt, n), BF16),
        compiler_params=pltpu.CompilerParams(
            dimension_semantics=("parallel", "arbitrary"), vmem_limit_bytes=VMEM_LIMIT),
        name="in_proj",
    )(x_bf, wt_bf)


def _ssd_kernel(z_ref, xs_ref, bc_ref, dt_ref, cwx_ref, cbx_ref, cwbc_ref, cbbc_ref,
                dtb_ref, a_ref, dsk_ref, nw_ref, exp_ref, tri_ref,
                shift_ref, y_ref, state_ref, cx_ref, cbc_ref, *, tb):
    @pl.when(pl.program_id(1) == 0)
    def _():
        state_ref[...] = jnp.zeros_like(state_ref)
        cx_ref[...] = jnp.zeros_like(cx_ref)
        cbc_ref[...] = jnp.zeros_like(cbc_ref)

    row8 = lax.broadcasted_iota(jnp.int32, (SUBLANES, 1), 0)

    def conv(u_ref, carry_ref, w_ref, b_ref):
        u_bf = u_ref[...]
        shifted = _dot(shift_ref[...], u_bf)
        u = u_bf.astype(F32)
        c8 = carry_ref[...]
        acc = u * w_ref[SSD_CONV - 1:SSD_CONV, :] + b_ref[...]
        for d in range(1, SSD_CONV):
            sh = shifted[(d - 1) * tb:d * tb]
            head = jnp.where(row8 < d, pltpu.roll(c8, d, 0), sh[0:SUBLANES])
            sh = jnp.concatenate([head, sh[SUBLANES:]], axis=0)
            acc = acc + sh * w_ref[SSD_CONV - 1 - d:SSD_CONV - d, :]
        carry_ref[...] = u[tb - SUBLANES:tb]
        return _silu(acc)

    xs = conv(xs_ref, cx_ref, cwx_ref, cbx_ref)
    bc = conv(bc_ref, cbc_ref, cwbc_ref, cbbc_ref)

    L = SSD_CHUNK
    ti = lax.broadcasted_iota(jnp.int32, (L, L), 0)
    si = lax.broadcasted_iota(jnp.int32, (L, L), 1)
    causal = ti >= si
    lane = lax.broadcasted_iota(jnp.int32, (1, LANES), 1)
    lo_half = lane < SSD_HEAD_DIM
    tri = tri_ref[...]
    expand = exp_ref[...]
    gsn = SSD_GROUPS * SSD_STATE

    for ci in range(tb // L):
        r0 = ci * L
        dt = _softplus(dt_ref[r0:r0 + L, :].astype(F32) + dtb_ref[...])
        da = dt * a_ref[...]
        cs = jnp.dot(tri, da, precision=HIGHEST, preferred_element_type=F32)
        cs_t = cs.T
        dt_t = dt.T
        e_exp = _dot_hilo(jnp.exp(cs), expand)
        for g in range(SSD_GROUPS):
            c0 = g * SSD_GROUP_W
            x_g = xs[r0:r0 + L, c0:c0 + SSD_GROUP_W]
            b_g = bc[r0:r0 + L, g * SSD_STATE:(g + 1) * SSD_STATE]
            c_g = bc[r0:r0 + L, gsn + g * SSD_STATE:gsn + (g + 1) * SSD_STATE]
            c_bf = c_g.astype(BF16)
            cb = _dot_nt(c_bf, b_g.astype(BF16))
            b_t = b_g.T
            st_old = state_ref[:, c0:c0 + SSD_GROUP_W]
            y_off = _dot(c_bf, st_old.astype(BF16))
            y_parts, new_parts = [], []
            for pr in range(2):
                xp = x_g[:, pr * LANES:(pr + 1) * LANES]
                x_half = (jnp.where(lo_half, xp, 0.0).astype(BF16), jnp.where(lo_half, 0.0, xp).astype(BF16))
                y_p = None
                n_p = None
                for k in range(2):
                    h = g * 4 + pr * 2 + k
                    col = cs[:, h:h + 1]
                    row = cs_t[h:h + 1, :]
                    dtrow = dt_t[h:h + 1, :]
                    lmat = jnp.exp(jnp.where(causal, col - row, -1e30))
                    m_h = (cb * lmat * dtrow).astype(BF16)
                    y_k = _dot(m_h, x_half[k])
                    wrow = jnp.exp(cs_t[h:h + 1, L - 1:L] - row) * dtrow
                    n_k = _dot((b_t * wrow).astype(BF16), x_half[k])
                    y_p = y_k if y_p is None else y_p + y_k
                    n_p = n_k if n_p is None else n_p + n_k
                y_parts.append(y_p)
                new_parts.append(n_p)
            y_diag = jnp.concatenate(y_parts, axis=1)
            new_g = jnp.concatenate(new_parts, axis=1)
            e_g = e_exp[:, c0:c0 + SSD_GROUP_W]
            state_ref[:, c0:c0 + SSD_GROUP_W] = st_old * e_g[L - 1:L, :] + new_g
            y = y_diag + y_off * e_g + dsk_ref[:, c0:c0 + SSD_GROUP_W] * x_g
            y = y * _silu(z_ref[r0:r0 + L, c0:c0 + SSD_GROUP_W].astype(F32))
            ms = jnp.mean(y * y, axis=-1, keepdims=True)
            y = y * lax.rsqrt(ms + SSD_NORM_EPS) * nw_ref[:, c0:c0 + SSD_GROUP_W]
            y_ref[r0:r0 + L, c0:c0 + SSD_GROUP_W] = y.astype(y_ref.dtype)


def _ssd_branch(proj_a, batch, seq, conv_w, conv_b, dt_bias, a_log, d_skip, norm_w, tb):
    nt = batch * seq
    nblk = seq // tb
    w2 = SSD_WIDTH // D_MODEL
    assert w2 == 1
    dt_blk = (2 * SSD_WIDTH + 2 * SSD_GROUPS * SSD_STATE) // LANES
    pad_h = LANES - SSD_HEADS
    dtb = jnp.pad(dt_bias, (0, pad_h)).reshape(1, LANES)
    a_neg = jnp.pad(-jnp.exp(a_log.astype(F32)), (0, pad_h)).reshape(1, LANES)
    dsk = jnp.repeat(d_skip, SSD_HEAD_DIM).reshape(1, SSD_WIDTH)
    expand = (jnp.arange(LANES)[:, None] == (jnp.arange(SSD_WIDTH)[None, :] // SSD_HEAD_DIM)).astype(BF16)
    tri = jnp.tril(jnp.ones((SSD_CHUNK, SSD_CHUNK), F32))
    t_idx = jnp.arange(tb)
    shift = jnp.concatenate([(t_idx[None, :] == t_idx[:, None] - d).astype(BF16) for d in range(1, SSD_CONV)],
                            axis=0)
    row = lambda b, i: (b * nblk + i, 0)
    full = lambda b, i: (0, 0)
    return pl.pallas_call(
        functools.partial(_ssd_kernel, tb=tb),
        grid=(batch, nblk),
        in_specs=[
            pl.BlockSpec((tb, SSD_WIDTH), lambda b, i: (b * nblk + i, 0)),
            pl.BlockSpec((tb, SSD_WIDTH), lambda b, i: (b * nblk + i, 1)),
            pl.BlockSpec((tb, SSD_WIDTH), lambda b, i: (b * nblk + i, 2)),
            pl.BlockSpec((tb, LANES), lambda b, i: (b * nblk + i, dt_blk)),
            pl.BlockSpec((SSD_CONV, SSD_WIDTH), lambda b, i: (0, 0)),
            pl.BlockSpec((1, SSD_WIDTH), lambda b, i: (0, 0)),
            pl.BlockSpec((SSD_CONV, SSD_WIDTH), lambda b, i: (0, 1)),
            pl.BlockSpec((1, SSD_WIDTH), lambda b, i: (0, 1)),
            pl.BlockSpec((1, LANES), full), pl.BlockSpec((1, LANES), full),
            pl.BlockSpec((1, SSD_WIDTH), full), pl.BlockSpec((1, SSD_WIDTH), full),
            pl.BlockSpec((LANES, SSD_WIDTH), full), pl.BlockSpec((SSD_CHUNK, SSD_CHUNK), full),
            pl.BlockSpec(((SSD_CONV - 1) * tb, tb), full),
        ],
        out_specs=pl.BlockSpec((tb, SSD_WIDTH), row),
        out_shape=jax.ShapeDtypeStruct((nt, SSD_WIDTH), BF16),
        scratch_shapes=[pltpu.VMEM((SSD_STATE, SSD_WIDTH), F32),
                        pltpu.VMEM((SUBLANES, SSD_WIDTH), F32),
                        pltpu.VMEM((SUBLANES, SSD_WIDTH), F32)],
        compiler_params=pltpu.CompilerParams(
            dimension_semantics=("parallel", "arbitrary"), vmem_limit_bytes=VMEM_LIMIT),
        name="ssd_branch",
    )(proj_a, proj_a, proj_a, proj_a, conv_w, conv_b.reshape(1, -1), conv_w, conv_b.reshape(1, -1),
      dtb, a_neg, dsk, norm_w.reshape(1, -1), expand, tri, shift)


S5_QUADS = 4
S5_BLOCKS = 2 * S5_GROUPS * S5_STATE // LANES
S5_QBLK = S5_BLOCKS // S5_QUADS


def _s5_kernel(u_ref, z_ref, bb_ref, co_ref, lr_ref, li_ref, d_ref, wglu_ref, bglu_ref,
               y_ref, scr_ref, st_ref, *, tb, tpad, nbatch):
    @pl.when(pl.program_id(0) == 0)
    def _():
        st_ref[...] = jnp.zeros_like(st_ref)

    half_blk = S5_OCT_STATE // LANES
    nvec = 2 * S5_QUADS
    for b in range(nbatch):
        u_bf = u_ref[b]
        for o in range(S5_OCTETS):
            bu = _dot(u_bf[:, o * LANES:(o + 1) * LANES], bb_ref[o])
            q, half = divmod(o, 2)
            for j in range(half_blk):
                jr = S5_QBLK * q + half_blk * half + j
                ji = jr + SUBLANES
                scr_ref[b, jr * tpad:jr * tpad + tb, :] = bu[:, j * LANES:(j + 1) * LANES]
                scr_ref[b, ji * tpad:ji * tpad + tb, :] = bu[:, S5_OCT_STATE + j * LANES:
                                                              S5_OCT_STATE + (j + 1) * LANES]

    lr = [lr_ref[q] for q in range(S5_QUADS)]
    li = [li_ref[q] for q in range(S5_QUADS)]

    def step(t, carry):
        new = []
        for b in range(nbatch):
            for q in range(S5_QUADS):
                sr, si = carry[nvec * b + 2 * q], carry[nvec * b + 2 * q + 1]
                re_rows = pl.ds(S5_QBLK * q * tpad + t, SUBLANES, stride=tpad)
                im_rows = pl.ds((S5_QBLK * q + SUBLANES) * tpad + t, SUBLANES, stride=tpad)
                nr = lr[q] * sr - li[q] * si + scr_ref[b, re_rows, :]
                ni = lr[q] * si + li[q] * sr + scr_ref[b, im_rows, :]
                scr_ref[b, re_rows, :] = nr
                scr_ref[b, im_rows, :] = ni
                new += [nr, ni]
        return tuple(new)

    carry = lax.fori_loop(0, tb, step, tuple(st_ref[i] for i in range(nbatch * nvec)), unroll=8)
    for i in range(nbatch * nvec):
        st_ref[i] = carry[i]

    for b in range(nbatch):
        ys = []
        for o in range(S5_OCTETS):
            q, half = divmod(o, 2)
            blks = []
            for part in range(2):
                for j in range(half_blk):
                    jb = S5_QBLK * q + SUBLANES * part + half_blk * half + j
                    blks.append(scr_ref[b, jb * tpad:jb * tpad + tb, :].astype(BF16))
            ys.append(_dot(jnp.concatenate(blks, axis=1), co_ref[o]))
        y = jnp.concatenate(ys, axis=1) + d_ref[...] * u_ref[b].astype(F32)
        y = 0.5 * y * (1.0 + jnp.tanh(0.7978845608028654 * (y + 0.044715 * (y * y * y))))
        gl = _dot(y.astype(BF16), wglu_ref[...]) + bglu_ref[...]
        y = y * _sigmoid(gl)
        y_ref[b] = (y * _silu(z_ref[b].astype(F32))).astype(y_ref.dtype)


def _s5_branch(proj_b, batch, seq, lam_re, lam_im, log_dt, b_re, b_im, c_re, c_im, d_s5, w_glu, b_glu, tb):
    nt = batch * seq
    nblk = seq // tb
    tpad = tb + SUBLANES // 2
    dt = jnp.exp(log_dt)[:, None]
    mag = jnp.exp(lam_re * dt)
    ang = lam_im * dt
    lb_re, lb_im = mag * jnp.cos(ang), mag * jnp.sin(ang)
    den = jnp.square(lam_re) + jnp.square(lam_im)
    nr, ni = lb_re - 1.0, lb_im
    q_re = (nr * lam_re + ni * lam_im) / den
    q_im = (ni * lam_re - nr * lam_im) / den
    bb_re = q_re[..., None] * b_re - q_im[..., None] * b_im
    bb_im = q_re[..., None] * b_im + q_im[..., None] * b_re
    g8 = LANES // S5_GROUP_CH
    eye = jnp.eye(g8, dtype=F32)

    def in_mat(bb):
        m = jnp.einsum('ogph,gk->oghkp', bb.reshape(S5_OCTETS, g8, S5_STATE, S5_GROUP_CH), eye)
        return m.reshape(S5_OCTETS, LANES, S5_OCT_STATE)

    def out_mat(c):
        m = jnp.einsum('oghp,gk->okpgh', c.reshape(S5_OCTETS, g8, S5_GROUP_CH, S5_STATE), eye)
        return m.reshape(S5_OCTETS, S5_OCT_STATE, LANES)

    bb = jnp.concatenate([in_mat(bb_re), in_mat(bb_im)], axis=2).astype(BF16)
    co = jnp.concatenate([out_mat(c_re), -out_mat(c_im)], axis=1).astype(BF16)
    lr = lb_re.reshape(S5_QUADS, SUBLANES, LANES)
    li = lb_im.reshape(S5_QUADS, SUBLANES, LANES)
    full2 = lambda i: (0, 0)
    full3 = lambda i: (0, 0, 0)
    proj3 = proj_b.reshape(batch, seq, proj_b.shape[1])
    out = pl.pallas_call(
        functools.partial(_s5_kernel, tb=tb, tpad=tpad, nbatch=batch),
        grid=(nblk,),
        in_specs=[
            pl.BlockSpec((batch, tb, S5_WIDTH), lambda i: (0, i, 0)),
            pl.BlockSpec((batch, tb, S5_WIDTH), lambda i: (0, i, 1)),
            pl.BlockSpec((S5_OCTETS, LANES, 2 * S5_OCT_STATE), full3),
            pl.BlockSpec((S5_OCTETS, 2 * S5_OCT_STATE, LANES), full3),
            pl.BlockSpec((S5_QUADS, SUBLANES, LANES), full3),
            pl.BlockSpec((S5_QUADS, SUBLANES, LANES), full3),
            pl.BlockSpec((1, S5_WIDTH), full2),
            pl.BlockSpec((S5_WIDTH, S5_WIDTH), full2),
            pl.BlockSpec((1, S5_WIDTH), full2),
        ],
        out_specs=pl.BlockSpec((batch, tb, S5_WIDTH), lambda i: (0, i, 0)),
        out_shape=jax.ShapeDtypeStruct((batch, seq, S5_WIDTH), BF16),
        scratch_shapes=[pltpu.VMEM((batch, S5_BLOCKS * tpad, LANES), F32),
                        pltpu.VMEM((batch * 2 * S5_QUADS, SUBLANES, LANES), F32)],
        compiler_params=pltpu.CompilerParams(
            dimension_semantics=("arbitrary",), vmem_limit_bytes=VMEM_LIMIT),
        name="s5_branch",
    )(proj3, proj3, bb, co, lr, li, d_s5.reshape(1, -1), w_glu.astype(BF16), b_glu.reshape(1, -1))
    return out.reshape(nt, S5_WIDTH)


def _dot_rhs_hilo(m_bf, x):
    hi = x.astype(BF16)
    lo = (x - hi.astype(F32)).astype(BF16)
    return _dot(m_bf, hi) + _dot(m_bf, lo)


def _rwkv_kernel(r_ref, k_ref, v_ref, z_ref, lo_ref, mu_ref, mulo_ref, wl_ref, w0_ref, a0_ref, kk_ref, ka_ref,
                 rk_ref, lw_ref, lb_ref, ones_ref, tri_ref, cones_ref, y_ref, st_ref, carry_ref, *, tb, nbatch,
                 npp):
    @pl.when(pl.program_id(1) == 0)
    def _():
        st_ref[...] = jnp.zeros_like(st_ref)
        carry_ref[...] = jnp.zeros_like(carry_ref)

    L = RWKV_CHUNK
    hd = RWKV_HEAD_DIM
    row8 = lax.broadcasted_iota(jnp.int32, (SUBLANES, 1), 0)
    lane = lax.broadcasted_iota(jnp.int32, (1, LANES), 1)
    head_mask = (lane < hd, lane >= hd)
    ones = ones_ref[...]

    def lerp(cur, cslot, mu):
        prev = _shift_rows(cur, carry_ref[cslot], 1, row8)
        carry_ref[cslot] = cur[tb - SUBLANES:tb]
        return cur + (prev - cur) * mu

    seqs = [(b, p) for b in range(nbatch) for p in range(npp)]
    sq = []
    for b in range(nbatch):
        xl = lerp(lo_ref[b].astype(F32), (b, 4, 0), mulo_ref[...])
        lin = jnp.where(head_mask[0], jnp.tanh(xl), xl).astype(BF16)
        for p in range(npp):
            ls = slice(p * LANES, (p + 1) * LANES)
            r = lerp(r_ref[b, :, ls].astype(F32), (b, 0, p), mu_ref[0:1, ls])
            k = lerp(k_ref[b, :, ls].astype(F32), (b, 1, p), mu_ref[1:2, ls])
            v = lerp(v_ref[b, :, ls].astype(F32), (b, 2, p), mu_ref[2:3, ls])
            zc = lerp(z_ref[b, :, ls].astype(F32), (b, 3, p), mu_ref[3:4, ls])
            wa = _dot(lin, wl_ref[ls, :])
            w_log = -_softplus(-(w0_ref[:, ls] + wa[:, :LANES])) - 0.5
            a = _sigmoid(a0_ref[:, ls] + wa[:, LANES:])
            kkr = k * kk_ref[:, ls]
            kk = kkr * lax.rsqrt(jnp.maximum(_dot((kkr * kkr).astype(BF16), ones), 1e-24))
            sq.append(dict(r=r, v=v, zc=zc, k2=k * (1.0 + (a - 1.0) * ka_ref[:, ls]), av=-kk, bv=kk * a,
                           logdec=-jnp.exp(w_log)))

    sbr = RWKV_SUPER
    nsb = tb // sbr
    ti = lax.broadcasted_iota(jnp.int32, (sbr, sbr), 0)
    si = lax.broadcasted_iota(jnp.int32, (sbr, sbr), 1)
    same_chunk = (ti // L) == (si // L)
    strict = same_chunk & (si < ti)
    incl = same_chunk & (si <= ti)
    ri = lax.broadcasted_iota(jnp.int32, (LANES, LANES), 0)
    cj = lax.broadcasted_iota(jnp.int32, (LANES, LANES), 1)
    blockdiag2 = jnp.concatenate([(ri // hd) == (cj // hd)] * 2, axis=1)
    eye = ri == cj
    tri = tri_ref[...]
    cones = cones_ref[...]
    zeros_l = jnp.zeros((L, LANES), F32)

    units = [(s, sb) for s in range(len(seqs)) for sb in range(nsb)]
    pre = {}
    for u in units:
        s, sb = u
        q = sq[s]
        sl = slice(sb * sbr, (sb + 1) * sbr)
        ld = q['logdec'][sl]
        cum = _dot_rhs_hilo(tri, ld)
        ctot = _dot_rhs_hilo(cones, ld)
        e_neg = jnp.exp(-cum)
        e_end = jnp.exp(ctot - cum)
        pre[u] = dict(rt=q['r'][sl] * jnp.exp(cum), at=q['av'][sl] * jnp.exp(cum - ld), kt=q['k2'][sl] * e_neg,
                      bt=q['bv'][sl] * e_neg, kh=q['k2'][sl] * e_end, bh=q['bv'][sl] * e_end, v=q['v'][sl],
                      ctot=ctot)

    chains = [(u, h) for u in units for h in range(2)]
    xs, ps, arb, ark, vh = {}, {}, {}, {}, {}
    for c in chains:
        sb, h = c
        d = pre[sb]
        m = head_mask[h]
        at_h = jnp.where(m, d['at'], 0.0)
        lhs = jnp.concatenate([at_h, jnp.where(m, d['rt'], 0.0)], axis=0).astype(BF16)
        rhs1 = jnp.concatenate([d['bt'], d['kt']], axis=0).astype(BF16)
        a4 = _dot_nt(lhs, rhs1)
        ps[c] = jnp.where(strict, a4[0:sbr, 0:sbr], 0.0)
        aak = jnp.where(strict, a4[0:sbr, sbr:], 0.0).astype(BF16)
        arb[c] = jnp.where(incl, a4[sbr:, 0:sbr], 0.0).astype(BF16)
        ark[c] = jnp.where(incl, a4[sbr:, sbr:], 0.0).astype(BF16)
        vh[c] = jnp.where(m, d['v'], 0.0).astype(BF16)
        xs[c] = jnp.concatenate([at_h, _dot(aak, vh[c])], axis=1)
    n_dbl = L.bit_length() - 1
    for i in range(n_dbl):
        for c in chains:
            p_bf = ps[c].astype(BF16)
            xs[c] = xs[c] + _dot(p_bf, xs[c].astype(BF16))
            if i + 1 < n_dbl:
                ps[c] = _dot(p_bf, p_bf)
    w_sb, u_sb, q_sb, y0_sb = {}, {}, {}, {}
    for sb in units:
        w_s = xs[(sb, 0)][:, :LANES] + xs[(sb, 1)][:, :LANES]
        u_s = xs[(sb, 0)][:, LANES:] + xs[(sb, 1)][:, LANES:]
        q_s = pre[sb]['rt']
        y0_s = None
        for h in range(2):
            c = (sb, h)
            qy = _dot(arb[c], xs[c].astype(BF16))
            y0_h = qy[:, LANES:] + _dot(ark[c], vh[c])
            q_s = q_s + qy[:, :LANES]
            y0_s = y0_h if y0_s is None else y0_s + y0_h
        w_sb[sb] = w_s
        u_sb[sb] = u_s
        q_sb[sb] = q_s
        y0_sb[sb] = y0_s

    nchunk = tb // L
    mns, gcols = {}, {}
    for ci in range(nchunk):
        sbi, lo_r = divmod(ci * L, sbr)
        cs = slice(lo_r, lo_r + L)
        for s in range(len(seqs)):
            u = (s, sbi)
            d = pre[u]
            rhs2 = jnp.concatenate([jnp.concatenate([w_sb[u][cs], u_sb[u][cs]], axis=1),
                                    jnp.concatenate([zeros_l, d['v'][cs]], axis=1)], axis=0).astype(BF16)
            lhs3 = jnp.concatenate([d['bh'][cs], d['kh'][cs]], axis=0).T.astype(BF16)
            mns[(s, ci)] = jnp.where(blockdiag2, _dot(lhs3, rhs2), 0.0)
            gl = jnp.exp(d['ctot'][lo_r:lo_r + 1, :])
            gcols[(s, ci)] = jnp.sum(jnp.where(eye, gl, 0.0), axis=1, keepdims=True)

    states = [st_ref[s] for s in range(len(seqs))]
    ys = [[] for _ in seqs]
    for ci in range(nchunk):
        sbi, lo_r = divmod(ci * L, sbr)
        cs = slice(lo_r, lo_r + L)
        for s in range(len(seqs)):
            u = (s, sbi)
            st = states[s]
            ys[s].append(_dot(q_sb[u][cs].astype(BF16), st.astype(BF16)) + y0_sb[u][cs])
            mn = mns[(s, ci)]
            states[s] = gcols[(s, ci)] * st + _dot_rhs_hilo(mn[:, :LANES].astype(BF16), st) + mn[:, LANES:]

    inv_hd = 1.0 / hd
    for s, (b, p) in enumerate(seqs):
        st_ref[s] = states[s]
        q = sq[s]
        ls = slice(p * LANES, (p + 1) * LANES)
        y = jnp.concatenate(ys[s], axis=0)
        mean = _dot_hilo(y, ones) * inv_hd
        yc = y - mean
        var = _dot((yc * yc).astype(BF16), ones) * inv_hd
        yn = yc * lax.rsqrt(var + RWKV_LNX_EPS) * lw_ref[:, ls] + lb_ref[:, ls]
        rk = _dot((q['r'] * q['k2'] * rk_ref[:, ls]).astype(BF16), ones)
        y_ref[b, :, ls] = ((yn + rk * q['v']) * _silu(q['zc'])).astype(y_ref.dtype)


def _rwkv_branch(proj_c, batch, seq, mu, mu_lora, w0, w2, a0, a2, k_k, k_a, r_k, lnx_w, lnx_b, tb, npp):
    nt = batch * seq
    nblk = seq // tb
    npair = RWKV_WIDTH // LANES
    wblk = npp * LANES
    cb = RWKV_WIDTH // wblk
    proj3 = proj_c.reshape(batch, seq, proj_c.shape[1])
    w2p = w2.reshape(RWKV_LORA, npair, LANES).transpose(1, 0, 2)
    a2p = a2.reshape(RWKV_LORA, npair, LANES).transpose(1, 0, 2)
    zer = jnp.zeros_like(w2p)
    wl = jnp.concatenate([jnp.concatenate([w2p, zer], axis=2), jnp.concatenate([zer, a2p], axis=2)], axis=1)
    wl = wl.reshape(npair * LANES, 2 * LANES).astype(BF16)
    ones = (jnp.arange(LANES)[:, None] // RWKV_HEAD_DIM == jnp.arange(LANES)[None, :] // RWKV_HEAD_DIM).astype(BF16)
    idx = jnp.arange(RWKV_SUPER)
    same_chunk = idx[:, None] // RWKV_CHUNK == idx[None, :] // RWKV_CHUNK
    cones = same_chunk.astype(BF16)
    tri = (same_chunk & (idx[None, :] <= idx[:, None])).astype(BF16)
    vec = lambda p: p.reshape(1, RWKV_WIDTH)
    tok = lambda off: pl.BlockSpec((batch, tb, wblk), lambda h, i: (0, i, off + h))
    pvec = pl.BlockSpec((1, wblk), lambda h, i: (0, h))
    full = lambda h, i: (0, 0)
    out = pl.pallas_call(
        functools.partial(_rwkv_kernel, tb=tb, nbatch=batch, npp=npp),
        grid=(npair // npp, nblk),
        in_specs=[
            tok(0), tok(cb), tok(2 * cb), tok(3 * cb),
            pl.BlockSpec((batch, tb, LANES), lambda h, i: (0, i, 4 * RWKV_WIDTH // LANES)),
            pl.BlockSpec((4, wblk), lambda h, i: (0, h)),
            pl.BlockSpec((1, LANES), full),
            pl.BlockSpec((wblk, 2 * LANES), lambda h, i: (h, 0)),
            pvec, pvec, pvec, pvec, pvec, pvec, pvec,
            pl.BlockSpec((LANES, LANES), full),
            pl.BlockSpec((RWKV_SUPER, RWKV_SUPER), full),
            pl.BlockSpec((RWKV_SUPER, RWKV_SUPER), full),
        ],
        out_specs=pl.BlockSpec((batch, tb, wblk), lambda h, i: (0, i, h)),
        out_shape=jax.ShapeDtypeStruct((batch, seq, RWKV_WIDTH), BF16),
        scratch_shapes=[pltpu.VMEM((batch * npp, LANES, LANES), F32),
                        pltpu.VMEM((batch, 5, npp, SUBLANES, LANES), F32)],
        compiler_params=pltpu.CompilerParams(
            dimension_semantics=("parallel", "arbitrary"), vmem_limit_bytes=VMEM_LIMIT),
        name="rwkv_branch",
    )(proj3, proj3, proj3, proj3, proj3, mu, mu_lora.reshape(1, 2 * RWKV_LORA), wl, vec(w0), vec(a0),
      vec(k_k), vec(k_a), vec(r_k), vec(lnx_w), vec(lnx_b), ones, tri, cones)
    return out.reshape(nt, RWKV_WIDTH)


def _merge_kernel(x_ref, ya_ref, yb_ref, yc_ref, wg0_ref, wg1_ref, wg2_ref, wa_ref, wb_ref, wc_ref, gb_ref, o_ref):
    x = x_ref[...]
    m = _sigmoid(_dot_nt(x, wg0_ref[...]) + gb_ref[0:1, :]) * _dot(ya_ref[...], wa_ref[...])
    m = m + _sigmoid(_dot_nt(x, wg1_ref[...]) + gb_ref[1:2, :]) * _dot(yb_ref[...], wb_ref[...])
    m = m + _sigmoid(_dot_nt(x, wg2_ref[...]) + gb_ref[2:3, :]) * _dot(yc_ref[...], wc_ref[...])
    o_ref[...] = m.astype(o_ref.dtype)


def _merge(x_bf, ya, yb, yc, wg, gate_b, wa, wb, wc, layer, tm, tn):
    nt, d = x_bf.shape
    nj = d // tn
    act = lambda w: pl.BlockSpec((tm, w), lambda i, j: (i, 0))
    wcol = lambda k, off: pl.BlockSpec((None, k, tn), lambda i, j: (layer, 0, off + j))
    wrow = lambda off: pl.BlockSpec((None, tn, d), lambda i, j: (layer, off + j, 0))
    return pl.pallas_call(
        _merge_kernel,
        grid=(nt // tm, nj),
        in_specs=[act(d), act(SSD_WIDTH), act(S5_WIDTH), act(RWKV_WIDTH),
                  wrow(0), wrow(nj), wrow(2 * nj),
                  wcol(SSD_WIDTH, 0), wcol(S5_WIDTH, 0), wcol(RWKV_WIDTH, 0),
                  pl.BlockSpec((N_BRANCHES, tn), lambda i, j: (0, j))],
        out_specs=pl.BlockSpec((tm, tn), lambda i, j: (i, j)),
        out_shape=jax.ShapeDtypeStruct((nt, d), BF16),
        compiler_params=pltpu.CompilerParams(
            dimension_semantics=("parallel", "arbitrary"), vmem_limit_bytes=VMEM_LIMIT),
        name="gated_merge",
    )(x_bf, ya, yb, yc, wg, wg, wg, wa, wb, wc, gate_b)


def _out_kernel(m_ref, x_ref, wo_ref, g_ref, b_ref, xo_ref, xb_ref):
    h = DEEPNORM_ALPHA * x_ref[...] + _dot(m_ref[...], wo_ref[...])
    mu = jnp.mean(h, axis=-1, keepdims=True)
    hc = h - mu
    var = jnp.mean(hc * hc, axis=-1, keepdims=True)
    y = hc * lax.rsqrt(var + LN_EPS) * g_ref[...] + b_ref[...]
    xo_ref[...] = y
    xb_ref[...] = y.astype(BF16)


def _out_norm(merged, x_f32, wo, ln_g, ln_b, layer, tm):
    nt, d = x_f32.shape
    row = pl.BlockSpec((tm, d), lambda i: (i, 0))
    vec = pl.BlockSpec((1, d), lambda i: (0, 0))
    return pl.pallas_call(
        _out_kernel,
        grid=(nt // tm,),
        in_specs=[row, row, pl.BlockSpec((None, d, d), lambda i: (layer, 0, 0)), vec, vec],
        out_specs=[row, row],
        out_shape=[jax.ShapeDtypeStruct((nt, d), F32), jax.ShapeDtypeStruct((nt, d), BF16)],
        compiler_params=pltpu.CompilerParams(
            dimension_semantics=("parallel",), vmem_limit_bytes=VMEM_LIMIT),
        name="out_norm",
    )(merged, x_f32, wo, ln_g.reshape(1, d), ln_b.reshape(1, d))


_Tiles = collections.namedtuple("_Tiles", "tm tn_a tn_b tn_c tn_merge tm_out tb_ssd tb_s5 tb_rwkv npp_rwkv")


def _tiles(nt, seq):
    return _Tiles(tm=min(1024, nt), tn_a=5 * MXU_WIDTH, tn_b=4 * MXU_WIDTH, tn_c=6 * MXU_WIDTH,
                  tn_merge=MXU_WIDTH, tm_out=min(512, nt), tb_ssd=min(256, seq), tb_s5=min(256, seq),
                  tb_rwkv=min(RWKV_SUPER, seq), npp_rwkv=4)


def kernel(x, w_in, ssd_conv_w, ssd_conv_b, ssd_dt_bias, ssd_a_log, ssd_d, ssd_norm_w, s5_lambda_re, s5_lambda_im, s5_log_dt, s5_b_re, s5_b_im, s5_c_re, s5_c_im, s5_d, s5_w_glu, s5_b_glu, rwkv_mu, rwkv_mu_lora, rwkv_w0, rwkv_w2, rwkv_a0, rwkv_a2, rwkv_k_k, rwkv_k_a, rwkv_r_k, rwkv_lnx_w, rwkv_lnx_b, gate_b, w_branch_a, w_branch_b, w_branch_c, w_out, ln_g, ln_b):
    b, s, d = x.shape
    nt = b * s
    t = _tiles(nt, s)
    xf = x.reshape(nt, d)
    xb = xf.astype(BF16)
    wt = jnp.swapaxes(w_in, 1, 2).astype(BF16)
    w_a = wt[:, OFF_A:OFF_A + LEN_A + PAD_A]
    w_b = wt[:, OFF_B:OFF_B + LEN_B]
    w_c = wt[:, OFF_C:OFF_C + LEN_C]
    w_g = wt[:, OFF_G:OFF_G + LEN_G]
    wb_a, wb_b, wb_c = w_branch_a.astype(BF16), w_branch_b.astype(BF16), w_branch_c.astype(BF16)
    w_o = w_out.astype(BF16)
    for l in range(DEPTH):
        pa = _project(xb, w_a, l, t.tm, t.tn_a)
        pb = _project(xb, w_b, l, t.tm, t.tn_b)
        pc = _project(xb, w_c, l, t.tm, t.tn_c)
        ya = _ssd_branch(pa, b, s, ssd_conv_w[l], ssd_conv_b[l], ssd_dt_bias[l], ssd_a_log[l], ssd_d[l],
                         ssd_norm_w[l], t.tb_ssd)
        yb = _s5_branch(pb, b, s, s5_lambda_re[l], s5_lambda_im[l], s5_log_dt[l], s5_b_re[l], s5_b_im[l],
                        s5_c_re[l], s5_c_im[l], s5_d[l], s5_w_glu[l], s5_b_glu[l], t.tb_s5)
        yc = _rwkv_branch(pc, b, s, rwkv_mu[l], rwkv_mu_lora[l], rwkv_w0[l], rwkv_w2[l], rwkv_a0[l], rwkv_a2[l],
                          rwkv_k_k[l], rwkv_k_a[l], rwkv_r_k[l], rwkv_lnx_w[l], rwkv_lnx_b[l], t.tb_rwkv,
                          t.npp_rwkv)
        merged = _merge(xb, ya, yb, yc, w_g, gate_b[l], wb_a, wb_b, wb_c, l, t.tm, t.tn_merge)
        xf, xb = _out_norm(merged, xf, w_o, ln_g[l], ln_b[l], l, t.tm_out)
    return xf.reshape(b, s, d)
```
